```python
import math
import jax, jax.numpy as jnp
from jax import lax
import numpy as np

D_MODEL = 4096
BATCH = 4
SEQ = 2048
DEPTH = 2
DEC_BATCH = 128
DEC_SEQ = 1
PAST_LEN = 16384
PAGE_SIZE = 128

EPS = 1e-6
D_FF = 11008
N_BRANCH = 4
BRANCH_WIDTH = 1024
A_WIDTH = BRANCH_WIDTH
A_GROUPS = 4
A_CHUNK = 128
B_HEADS = 4
B_DK = 128
B_DV = 256
B_WIDTH = B_HEADS * B_DV
B_GATE_RANK = 16
B_GATE_TAU = 16.0
B_CHUNK = 64
C_WIDTH = BRANCH_WIDTH
CONV_WIDTH = 31
D_HEADS = 8
D_NOPE = 128
D_ROPE = 64
D_VDIM = 128
D_WIDTH = D_HEADS * D_VDIM
D_Q_RANK = 768
D_KV_RANK = 256
ROPE_BASE = 10000.0
Q_BLOCK = 128
NEG_INF = -1e30

IN_SIZES = (A_WIDTH, A_WIDTH, B_HEADS * B_DK, B_HEADS * B_DK, B_WIDTH, B_GATE_RANK, B_WIDTH,
            C_WIDTH, C_WIDTH, D_Q_RANK, D_KV_RANK, D_ROPE, N_BRANCH * D_MODEL)

kernel_name = 'hybrid_gated_branch_decoder_step'


def rms_norm(x, g):
    xf = x.astype(jnp.float32)
    y = xf * lax.rsqrt(jnp.mean(xf * xf, axis=-1, keepdims=True) + EPS)
    return (y * g.astype(jnp.float32)).astype(x.dtype)


def layer_norm(x, g, b):
    xf = x.astype(jnp.float32)
    mu = jnp.mean(xf, axis=-1, keepdims=True)
    xc = xf - mu
    var = jnp.mean(xc * xc, axis=-1, keepdims=True)
    return (xc * lax.rsqrt(var + EPS) * g.astype(jnp.float32) + b.astype(jnp.float32)).astype(x.dtype)


def ffn_sublayer(h, g_pre, g_post, w_gu, w_down):
    n = rms_norm(h, g_pre)
    a, b = jnp.split(n @ w_gu, 2, axis=-1)
    f = (jax.nn.silu(a) * b) @ w_down
    return h + 0.5 * rms_norm(f, g_post)


def chunk_mlp(u, v, w_s, b_s):
    bsz, t, _ = v.shape
    c = min(t, A_CHUNK)
    n = t // c
    causal = jnp.tril(jnp.ones((c, c), dtype=bool))
    ws = jnp.where(causal, w_s[:, :c, :c], 0.0).astype(v.dtype)
    vg = v.reshape(bsz, n, c, A_GROUPS, A_WIDTH // A_GROUPS)
    mixed = jnp.einsum('gts,bnsgc->bntgc', ws, vg) + b_s[:, :c].T[None, None, :, :, None].astype(v.dtype)
    return u * mixed.reshape(bsz, t, A_WIDTH)


def gla_recurrence(q, k, v, log_a, s0):
    bsz, t = q.shape[:2]
    c = math.gcd(t, B_CHUNK)
    n = t // c

    def to_chunks(x):
        return x.reshape(bsz, n, c, *x.shape[2:]).swapaxes(0, 1).astype(jnp.float32)

    qc, kc, vc, ac = to_chunks(q), to_chunks(k), to_chunks(v), to_chunks(log_a)
    causal = jnp.tril(jnp.ones((c, c), dtype=bool))

    def step(s, inp):
        qi, ki, vi, ai = inp
        cum = jnp.cumsum(ai, axis=1)
        q_t = qi * jnp.exp(cum)
        k_t = ki * jnp.exp(-cum)
        scores = jnp.where(causal, jnp.einsum('bthd,bshd->bhts', q_t, k_t), 0.0)
        o = jnp.einsum('bhts,bshv->bthv', scores, vi) + jnp.einsum('bthd,bhdv->bthv', q_t, s)
        total = cum[:, -1]
        k_end = ki * jnp.exp(total[:, None] - cum)
        s_new = jnp.exp(total)[..., None] * s + jnp.einsum('bshd,bshv->bhdv', k_end, vi)
        return s_new, o

    s_final, o = lax.scan(step, s0.astype(jnp.float32), (qc, kc, vc, ac))
    o = o.swapaxes(0, 1).reshape(bsz, t, *v.shape[2:])
    return o.astype(v.dtype), s_final.astype(s0.dtype)


def causal_dwconv(x, buf, w, b):
    xp = jnp.concatenate([buf.astype(x.dtype), x], axis=1)
    y = lax.conv_general_dilated(xp, w[:, None, :].astype(x.dtype), window_strides=(1,), padding='VALID',
                                 dimension_numbers=('NWC', 'WIO', 'NWC'), feature_group_count=C_WIDTH)
    return y + b.astype(x.dtype), xp[:, -(CONV_WIDTH - 1):]


def apply_rope(x, pos):
    half = D_ROPE // 2
    inv_freq = ROPE_BASE ** (-jnp.arange(half, dtype=jnp.float32) / half)
    ang = pos.astype(jnp.float32)[:, None] * inv_freq[None, :]
    cos = jnp.cos(ang)[None, :, None, :]
    sin = jnp.sin(ang)[None, :, None, :]
    xf = x.astype(jnp.float32)
    x1, x2 = xf[..., :half], xf[..., half:]
    return jnp.concatenate([x1 * cos - x2 * sin, x2 * cos + x1 * sin], axis=-1).astype(x.dtype)


def mla_attend(q_lat, q_rope, q_pos, segments):
    scale = (D_NOPE + D_ROPE) ** -0.5
    scores = []
    for kv, kr, k_pos in segments:
        s = jnp.einsum('bqhc,bkc->bhqk', q_lat, kv) + jnp.einsum('bqhr,bkr->bhqk', q_rope, kr)
        s = jnp.where(k_pos[None, :] <= q_pos[:, None], s.astype(jnp.float32) * scale, NEG_INF)
        scores.append(s)
    probs = jax.nn.softmax(jnp.concatenate(scores, axis=-1), axis=-1)
    outs = []
    start = 0
    for kv, _, _ in segments:
        tk = kv.shape[1]
        outs.append(jnp.einsum('bhqk,bkc->bqhc', probs[..., start:start + tk].astype(kv.dtype), kv))
        start += tk
    return sum(outs)


def mla_blocks(q_lat, q_rope, q_pos, segments):
    tq = q_lat.shape[1]
    if tq <= Q_BLOCK or tq % Q_BLOCK:
        return mla_attend(q_lat, q_rope, q_pos, segments)
    nb = tq // Q_BLOCK

    def blocks(x):
        return x.reshape(x.shape[0], nb, Q_BLOCK, *x.shape[2:]).swapaxes(0, 1)

    o = lax.map(lambda blk: mla_attend(blk[0], blk[1], blk[2], segments),
                (blocks(q_lat), blocks(q_rope), q_pos.reshape(nb, Q_BLOCK)))
    return o.swapaxes(0, 1).reshape(q_lat.shape)


def run_layer(h, pos, conv_buf, gla_s0, past, p):
    h = ffn_sublayer(h, p['ffn1_norm_pre'], p['ffn1_norm_post'], p['ffn1_w_gu'], p['ffn1_w_down'])
    bsz, t, _ = h.shape
    n = rms_norm(h, p['mix_norm_pre'])
    proj = n @ p['w_in']
    offsets = np.cumsum(IN_SIZES)[:-1].tolist()
    (a_u, a_v, b_q, b_k, b_v, b_z, b_r, c_a, c_g, d_cq, d_ckv, d_kr, gate_logits) = jnp.split(proj, offsets, axis=-1)

    a_u = jax.nn.gelu(a_u)
    a_v = layer_norm(jax.nn.gelu(a_v), p['a_ln_g'], p['a_ln_b'])
    y_a = chunk_mlp(a_u, a_v, p['a_w_s'], p['a_b_s'])

    q = (b_q * B_DK ** -0.5).reshape(bsz, t, B_HEADS, B_DK)
    k = b_k.reshape(bsz, t, B_HEADS, B_DK)
    v = b_v.reshape(bsz, t, B_HEADS, B_DV)
    log_a = jax.nn.log_sigmoid((b_z @ p['b_w_alpha'] + p['b_b_alpha']).astype(jnp.float32)) / B_GATE_TAU
    o_b, gla_s = gla_recurrence(q, k, v, log_a.reshape(bsz, t, B_HEADS, B_DK), gla_s0)
    o_b = rms_norm(o_b, p['b_norm_g'].reshape(B_HEADS, B_DV))
    y_b = jax.nn.silu(b_r) * o_b.reshape(bsz, t, B_WIDTH)

    glu = c_a * jax.nn.sigmoid(c_g)
    conv, conv_new = causal_dwconv(glu, conv_buf, p['c_conv_w'], p['c_conv_b'])
    y_c = jax.nn.silu(layer_norm(conv, p['c_ln_g'], p['c_ln_b']))

    c_q = rms_norm(d_cq, p['d_q_norm_g'])
    qh = (c_q @ p['d_w_uq']).reshape(bsz, t, D_HEADS, D_NOPE + D_ROPE)
    q_rope = apply_rope(qh[..., D_NOPE:], pos)
    q_lat = jnp.einsum('bthn,chn->bthc', qh[..., :D_NOPE], p['d_w_uk'])
    c_kv = rms_norm(d_ckv, p['d_kv_norm_g'])
    k_rope = apply_rope(d_kr[:, :, None, :], pos)[:, :, 0, :]
    o_lat = mla_blocks(q_lat, q_rope, pos, past + ((c_kv, k_rope, pos),))
    y_d = jnp.einsum('bthc,chv->bthv', o_lat, p['d_w_uv']).reshape(bsz, t, D_WIDTH)

    ys = jnp.stack([y_a, y_b, y_c, y_d], axis=2)
    gates = jax.nn.sigmoid(gate_logits.astype(jnp.float32)).astype(h.dtype).reshape(bsz, t, N_BRANCH, D_MODEL)
    merged = jnp.sum(gates * jnp.einsum('btnc,ncd->btnd', ys, p['w_branch']), axis=2)
    h = h + rms_norm(merged @ p['w_out'], p['mix_norm_post'])

    h = ffn_sublayer(h, p['ffn2_norm_pre'], p['ffn2_norm_post'], p['ffn2_w_gu'], p['ffn2_w_down'])
    return h, (c_kv, k_rope, gla_s, conv_new, a_v)


def setup_inputs(seed: int = 0) -> dict:
    key = jax.random.key(seed)
    ks = iter(jax.random.split(key, 48))
    n_pages = PAST_LEN // PAGE_SIZE
    n_pool = (DEC_BATCH * n_pages * 5) // 4
    d_in = sum(IN_SIZES)

    def nrm(shape, scale):
        return jax.random.normal(next(ks), shape, jnp.float32) * scale

    def gain(shape):
        return 1.0 + nrm(shape, 0.02)

    page_table = jax.random.permutation(next(ks), n_pool)[:DEC_BATCH * n_pages].reshape(DEC_BATCH, n_pages).astype(jnp.int32)
    return {
        'x_prompt': nrm((BATCH, SEQ, D_MODEL), 1.0),
        'x_sample': nrm((DEC_BATCH, DEC_SEQ, D_MODEL), 1.0),
        'cache_latent': nrm((DEPTH, n_pool, PAGE_SIZE, D_KV_RANK), 1.0),
        'cache_k_rope': nrm((DEPTH, n_pool, PAGE_SIZE, D_ROPE), 1.0),
        'page_table': page_table,
        'state_gla': nrm((DEPTH, DEC_BATCH, B_HEADS, B_DK, B_DV), 1.0),
        'state_conv': nrm((DEPTH, DEC_BATCH, CONV_WIDTH - 1, C_WIDTH), 0.5),
        'ffn1_norm_pre': gain((DEPTH, D_MODEL)),
        'ffn1_norm_post': gain((DEPTH, D_MODEL)),
        'ffn1_w_gu': nrm((DEPTH, D_MODEL, 2 * D_FF), D_MODEL ** -0.5),
        'ffn1_w_down': nrm((DEPTH, D_FF, D_MODEL), D_FF ** -0.5),
        'mix_norm_pre': gain((DEPTH, D_MODEL)),
        'mix_norm_post': gain((DEPTH, D_MODEL)),
        'w_in': nrm((DEPTH, D_MODEL, d_in), D_MODEL ** -0.5),
        'a_ln_g': gain((DEPTH, A_WIDTH)),
        'a_ln_b': nrm((DEPTH, A_WIDTH), 0.02),
        'a_w_s': nrm((DEPTH, A_GROUPS, A_CHUNK, A_CHUNK), A_CHUNK ** -0.5),
        'a_b_s': 1.0 + nrm((DEPTH, A_GROUPS, A_CHUNK), 0.1),
        'b_w_alpha': nrm((DEPTH, B_GATE_RANK, B_HEADS * B_DK), B_GATE_RANK ** -0.5),
        'b_b_alpha': nrm((DEPTH, B_HEADS * B_DK), 0.1),
        'b_norm_g': gain((DEPTH, B_WIDTH)),
        'c_conv_w': nrm((DEPTH, CONV_WIDTH, C_WIDTH), CONV_WIDTH ** -0.5),
        'c_conv_b': nrm((DEPTH, C_WIDTH), 0.02),
        'c_ln_g': gain((DEPTH, C_WIDTH)),
        'c_ln_b': nrm((DEPTH, C_WIDTH), 0.02),
        'd_q_norm_g': gain((DEPTH, D_Q_RANK)),
        'd_kv_norm_g': gain((DEPTH, D_KV_RANK)),
        'd_w_uq': nrm((DEPTH, D_Q_RANK, D_HEADS * (D_NOPE + D_ROPE)), D_Q_RANK ** -0.5),
        'd_w_uk': nrm((DEPTH, D_KV_RANK, D_HEADS, D_NOPE), D_KV_RANK ** -0.5),
        'd_w_uv': nrm((DEPTH, D_KV_RANK, D_HEADS, D_VDIM), D_KV_RANK ** -0.5),
        'w_branch': nrm((DEPTH, N_BRANCH, BRANCH_WIDTH, D_MODEL), BRANCH_WIDTH ** -0.5),
        'w_out': nrm((DEPTH, D_MODEL, D_MODEL), D_MODEL ** -0.5),
        'ffn2_norm_pre': gain((DEPTH, D_MODEL)),
        'ffn2_norm_post': gain((DEPTH, D_MODEL)),
        'ffn2_w_gu': nrm((DEPTH, D_MODEL, 2 * D_FF), D_MODEL ** -0.5),
        'ffn2_w_down': nrm((DEPTH, D_FF, D_MODEL), D_FF ** -0.5),
    }


def reference(x_prompt, x_sample, cache_latent, cache_k_rope, page_table, state_gla, state_conv,
              ffn1_norm_pre, ffn1_norm_post, ffn1_w_gu, ffn1_w_down, mix_norm_pre, mix_norm_post, w_in,
              a_ln_g, a_ln_b, a_w_s, a_b_s, b_w_alpha, b_b_alpha, b_norm_g,
              c_conv_w, c_conv_b, c_ln_g, c_ln_b, d_q_norm_g, d_kv_norm_g, d_w_uq, d_w_uk, d_w_uv,
              w_branch, w_out, ffn2_norm_pre, ffn2_norm_post, ffn2_w_gu, ffn2_w_down):
    bsz_p, t_p, _ = x_prompt.shape
    dec_b, t_s, _ = x_sample.shape
    past_len = page_table.shape[1] * cache_latent.shape[2]
    pos_p = jnp.arange(t_p, dtype=jnp.int32)
    pos_s = past_len + jnp.arange(t_s, dtype=jnp.int32)
    past_pos = jnp.arange(past_len, dtype=jnp.int32)
    gla0_p = jnp.zeros((bsz_p, B_HEADS, B_DK, B_DV), state_gla.dtype)
    conv0_p = jnp.zeros((bsz_p, CONV_WIDTH - 1, C_WIDTH), x_prompt.dtype)

    hp, hs = x_prompt, x_sample
    lat_p, lat_s, kr_p, kr_s, gla_p, gla_s, conv_p, conv_s, v_s = [], [], [], [], [], [], [], [], []
    for l in range(DEPTH):
        p = {
            'ffn1_norm_pre': ffn1_norm_pre[l], 'ffn1_norm_post': ffn1_norm_post[l],
            'ffn1_w_gu': ffn1_w_gu[l], 'ffn1_w_down': ffn1_w_down[l],
            'mix_norm_pre': mix_norm_pre[l], 'mix_norm_post': mix_norm_post[l], 'w_in': w_in[l],
            'a_ln_g': a_ln_g[l], 'a_ln_b': a_ln_b[l], 'a_w_s': a_w_s[l], 'a_b_s': a_b_s[l],
            'b_w_alpha': b_w_alpha[l], 'b_b_alpha': b_b_alpha[l], 'b_norm_g': b_norm_g[l],
            'c_conv_w': c_conv_w[l], 'c_conv_b': c_conv_b[l], 'c_ln_g': c_ln_g[l], 'c_ln_b': c_ln_b[l],
            'd_q_norm_g': d_q_norm_g[l], 'd_kv_norm_g': d_kv_norm_g[l],
            'd_w_uq': d_w_uq[l], 'd_w_uk': d_w_uk[l], 'd_w_uv': d_w_uv[l],
            'w_branch': w_branch[l], 'w_out': w_out[l],
            'ffn2_norm_pre': ffn2_norm_pre[l], 'ffn2_norm_post': ffn2_norm_post[l],
            'ffn2_w_gu': ffn2_w_gu[l], 'ffn2_w_down': ffn2_w_down[l],
        }
        hp, st_p = run_layer(hp, pos_p, conv0_p, gla0_p, (), p)
        kv_past = cache_latent[l, page_table].reshape(dec_b, past_len, D_KV_RANK)
        kr_past = cache_k_rope[l, page_table].reshape(dec_b, past_len, D_ROPE)
        hs, st_s = run_layer(hs, pos_s, state_conv[l], state_gla[l], ((kv_past, kr_past, past_pos),), p)
        lat_p.append(st_p[0]); kr_p.append(st_p[1]); gla_p.append(st_p[2]); conv_p.append(st_p[3])
        lat_s.append(st_s[0]); kr_s.append(st_s[1]); gla_s.append(st_s[2]); conv_s.append(st_s[3]); v_s.append(st_s[4])

    latent_prompt = jnp.stack(lat_p)
    latent_sample = jnp.stack(lat_s)
    k_rope_prompt = jnp.stack(kr_p)
    k_rope_sample = jnp.stack(kr_s)
    gla_prompt = jnp.stack(gla_p)
    gla_sample = jnp.stack(gla_s)
    conv_prompt = jnp.stack(conv_p)
    conv_sample = jnp.stack(conv_s)
    chunk_v_sample = jnp.stack(v_s)
    return (hp, hs, latent_prompt, latent_sample, k_rope_prompt, k_rope_sample,
            gla_prompt, gla_sample, conv_prompt, conv_sample, chunk_v_sample)
```

```python
import functools

import jax
import jax.numpy as jnp
from jax import lax
from jax.experimental import pallas as pl
from jax.experimental.pallas import tpu as pltpu

F32 = jnp.float32
BF = jnp.bfloat16

EPS = 1e-6
N_BRANCH = 4
BRANCH_WIDTH = 1024
A_GROUPS = 4
A_CHUNK = 128
B_HEADS = 4
B_DK = 128
B_DV = 256
B_GATE_RANK = 16
B_GATE_TAU = 16.0
B_CHUNK = 64
CONV_WIDTH = 31
D_HEADS = 8
D_NOPE = 128
D_ROPE = 64
D_VDIM = 128
D_Q_RANK = 768
D_KV_RANK = 256
ROPE_BASE = 10000.0
NEG_INF = -1e30

LANES = 128
VMEM_LIMIT = 56 * 1024 * 1024
PAGES_PER_STEP = 16
CONV_HALO = 32

C_AU, C_AV, C_BQ, C_BK, C_BV, C_BR, C_CA, C_CG, C_DQ, C_DKV, C_GATE = (
    0, 1024, 2048, 2560, 3072, 4096, 5120, 6144, 7168, 7936, 8192)
S_KR, S_KRS, S_BZ, S_WIDTH = 0, 64, 128, 256


def _tile(n, target, mult):
    best = None
    for t in range(mult, min(n, target) + 1, mult):
        if n % t == 0:
            best = t
    assert best is not None, (n, target, mult)
    return best


def _dot(a, b):
    return jnp.dot(a, b, preferred_element_type=F32)


def _dot_nt(a, b):
    return lax.dot_general(a, b, (((1,), (1,)), ((), ())), preferred_element_type=F32)


def _dot_tn(a, b):
    return lax.dot_general(a, b, (((0,), (0,)), ((), ())), preferred_element_type=F32)


def _split3(x):
    hi = x.astype(BF)
    r1 = x - hi.astype(F32)
    mid = r1.astype(BF)
    lo = (r1 - mid.astype(F32)).astype(BF)
    return hi, mid, lo


def _rms(x, g):
    return x * lax.rsqrt(jnp.mean(x * x, axis=-1, keepdims=True) + EPS) * g


def _layer_norm(x, g, b):
    mu = jnp.mean(x, axis=-1, keepdims=True)
    xc = x - mu
    var = jnp.mean(xc * xc, axis=-1, keepdims=True)
    return xc * lax.rsqrt(var + EPS) * g + b


def _silu(x):
    return x * jax.nn.sigmoid(x)


def _gelu(x):
    return jax.nn.gelu(x, approximate=True)


def _log_sigmoid(x):
    return jnp.minimum(x, 0.0) - jnp.log1p(jnp.exp(-jnp.abs(x)))


def _pcall(kernel, *, grid, in_specs, out_specs, out_shape, scratch_shapes=(), name=None, grid_spec=None):
    params = pltpu.CompilerParams(dimension_semantics=("arbitrary",) * len(grid), vmem_limit_bytes=VMEM_LIMIT)
    if grid_spec is not None:
        return pl.pallas_call(kernel, grid_spec=grid_spec, out_shape=out_shape, compiler_params=params, name=name)
    return pl.pallas_call(kernel, grid=grid, in_specs=in_specs, out_specs=out_specs, out_shape=out_shape,
                          scratch_shapes=scratch_shapes, compiler_params=params, name=name)


def _norm_kernel(x_ref, g_ref, o_ref):
    o_ref[...] = _rms(x_ref[...], g_ref[...]).astype(BF)


def _norm_cast(h, g):
    m, d = h.shape
    rb = _tile(m, 416, 16)
    return _pcall(
        _norm_kernel, grid=(m // rb,),
        in_specs=[pl.BlockSpec((rb, d), lambda i: (i, 0)), pl.BlockSpec((1, d), lambda i: (0, 0))],
        out_specs=pl.BlockSpec((rb, d), lambda i: (i, 0)),
        out_shape=jax.ShapeDtypeStruct((m, d), BF), name="norm_cast")(h, g.reshape(1, d))


def _post_kernel(f_ref, h_ref, gp_ref, gn_ref, ho_ref, no_ref, *, scale):
    h = h_ref[...] + scale * _rms(f_ref[...], gp_ref[...])
    ho_ref[...] = h
    no_ref[...] = _rms(h, gn_ref[...]).astype(BF)


def _post(f, h, g_post, g_next, scale):
    m, d = h.shape
    rb = _tile(m, 208, 16)
    row = pl.BlockSpec((rb, d), lambda i: (i, 0))
    vec = pl.BlockSpec((1, d), lambda i: (0, 0))
    return _pcall(
        functools.partial(_post_kernel, scale=scale), grid=(m // rb,),
        in_specs=[row, row, vec, vec], out_specs=[row, row],
        out_shape=[jax.ShapeDtypeStruct((m, d), F32), jax.ShapeDtypeStruct((m, d), BF)],
        name="post_norm")(f, h, g_post.reshape(1, d), g_next.reshape(1, d))


def _mm_kernel(x_ref, w_ref, o_ref):
    o_ref[...] = _dot(x_ref[...], w_ref[...].astype(BF)).astype(o_ref.dtype)


def _mm(x, w, layer, out_dtype, name, single_buffer_x=False):
    m, k = x.shape
    n = w.shape[-1]
    tm = _tile(m, 832, 16)
    tn = _tile(n, 512, 2 * LANES) if n % (2 * LANES) == 0 else _tile(n, 512, LANES)
    if k > 8192:
        tn = _tile(n, 256, LANES)
    if layer is None:
        w_spec = pl.BlockSpec((k, tn), lambda i, j: (0, j))
    else:
        w_spec = pl.BlockSpec((None, k, tn), lambda i, j: (layer, 0, j))
    if single_buffer_x:
        x_spec = pl.BlockSpec((tm, k), lambda i, j: (i, 0), pipeline_mode=pl.Buffered(1))
    else:
        x_spec = pl.BlockSpec((tm, k), lambda i, j: (i, 0))
    return _pcall(
        _mm_kernel, grid=(m // tm, n // tn), in_specs=[x_spec, w_spec],
        out_specs=pl.BlockSpec((tm, tn), lambda i, j: (i, j)),
        out_shape=jax.ShapeDtypeStruct((m, n), out_dtype), name=name)(x, w)


def _mm_gu_kernel(x_ref, wg_ref, wu_ref, o_ref):
    x = x_ref[...]
    a = _dot(x, wg_ref[...].astype(BF))
    b = _dot(x, wu_ref[...].astype(BF))
    o_ref[...] = (_silu(a) * b).astype(BF)


def _mm_gu(x, w_gu, layer):
    m, k = x.shape
    f = w_gu.shape[-1] // 2
    tm = _tile(m, 832, 16)
    tn = _tile(f, 256, LANES)
    nj = f // tn
    return _pcall(
        _mm_gu_kernel, grid=(m // tm, nj),
        in_specs=[pl.BlockSpec((tm, k), lambda i, j: (i, 0)),
                  pl.BlockSpec((None, k, tn), lambda i, j: (layer, 0, j)),
                  pl.BlockSpec((None, k, tn), lambda i, j: (layer, 0, j + nj))],
        out_specs=pl.BlockSpec((tm, tn), lambda i, j: (i, j)),
        out_shape=jax.ShapeDtypeStruct((m, f), BF), name="ffn_gate_up")(x, w_gu, w_gu)


def _merge_kernel(ya_ref, yb_ref, yc_ref, yd_ref, w_ref, g0_ref, g1_ref, g2_ref, g3_ref, o_ref):
    acc = None
    for b, (y_ref, g_ref) in enumerate(zip((ya_ref, yb_ref, yc_ref, yd_ref), (g0_ref, g1_ref, g2_ref, g3_ref))):
        t = _dot(y_ref[...], w_ref[b].astype(BF)) * jax.nn.sigmoid(g_ref[...])
        acc = t if acc is None else acc + t
    o_ref[...] = acc.astype(BF)


def _merge(ys, w_branch, layer, proj, d):
    m = proj.shape[0]
    tm = _tile(m, 832, 16)
    tn = _tile(d, 512, LANES)
    y_spec = pl.BlockSpec((tm, BRANCH_WIDTH), lambda i, j: (i, 0))
    gate_specs = [pl.BlockSpec((tm, tn), functools.partial(lambda i, j, b: (i, (C_GATE + b * d) // tn + j), b=b))
                  for b in range(N_BRANCH)]
    return _pcall(
        _merge_kernel, grid=(m // tm, d // tn),
        in_specs=[y_spec] * N_BRANCH
        + [pl.BlockSpec((None, N_BRANCH, BRANCH_WIDTH, tn), lambda i, j: (layer, 0, 0, j))] + gate_specs,
        out_specs=pl.BlockSpec((tm, tn), lambda i, j: (i, j)),
        out_shape=jax.ShapeDtypeStruct((m, d), BF), name="gated_merge")(*ys, w_branch, proj, proj, proj, proj)


def _a_kernel(p_ref, lg_ref, lb_ref, ws_ref, bs_ref, y_ref, v_ref):
    u = _gelu(p_ref[:, :BRANCH_WIDTH])
    v = _layer_norm(_gelu(p_ref[:, BRANCH_WIDTH:]), lg_ref[...], lb_ref[...])
    v_ref[...] = v
    vb = v.astype(BF)
    gw = BRANCH_WIDTH // A_GROUPS
    for g in range(A_GROUPS):
        cols = slice(g * gw, (g + 1) * gw)
        mixed = _dot(ws_ref[g], vb[:, cols]) + bs_ref[g]
        y_ref[:, cols] = (u[:, cols] * mixed).astype(BF)


def _mixer_a(proj, n_prompt_chunks, ln_g, ln_b, w_s, b_s):
    m = proj.shape[0]
    c = A_CHUNK
    causal = jnp.tril(jnp.ones((c, c), dtype=bool))
    ws = jnp.stack([jnp.where(causal, w_s, 0.0),
                    w_s[:, :1, :1] * jnp.eye(c, dtype=F32)[None]]).astype(BF)
    bs = jnp.stack([b_s, jnp.broadcast_to(b_s[:, :1], b_s.shape)])[..., None]
    kind = lambda i: i // n_prompt_chunks
    y, v = _pcall(
        _a_kernel, grid=(m // c,),
        in_specs=[pl.BlockSpec((c, 2 * BRANCH_WIDTH), lambda i: (i, C_AU // (2 * BRANCH_WIDTH))),
                  pl.BlockSpec((1, BRANCH_WIDTH), lambda i: (0, 0)),
                  pl.BlockSpec((1, BRANCH_WIDTH), lambda i: (0, 0)),
                  pl.BlockSpec((None, A_GROUPS, c, c), lambda i: (kind(i), 0, 0, 0)),
                  pl.BlockSpec((None, A_GROUPS, c, 1), lambda i: (kind(i), 0, 0, 0))],
        out_specs=[pl.BlockSpec((c, BRANCH_WIDTH), lambda i: (i, 0)),
                   pl.BlockSpec((c, BRANCH_WIDTH), lambda i: (0, 0))],
        out_shape=[jax.ShapeDtypeStruct((m, BRANCH_WIDTH), BF), jax.ShapeDtypeStruct((c, BRANCH_WIDTH), F32)],
        name="mixer_a")(proj, ln_g.reshape(1, -1), ln_b.reshape(1, -1), ws, bs)
    return y, v


def _gla_log_decay(z_ref, wa_ref, ba_ref, rows):
    z = z_ref[rows, :].astype(BF)
    return _log_sigmoid(_dot(z, wa_ref[...]) + ba_ref[...]) * (1.0 / B_GATE_TAU)


def _gla_out(o, g, r):
    return _silu(r) * _rms(o, g)


def _gla_p_kernel(q_ref, k_ref, v_ref, r_ref, z_ref, wa_ref, ba_ref, g_ref, y_ref, so_ref, s_scr, *, n_chunks):
    blk = pl.program_id(1)
    c = B_CHUNK

    @pl.when(blk == 0)
    def _():
        s_scr[...] = jnp.zeros_like(s_scr)

    ri = lax.broadcasted_iota(jnp.int32, (c, c), 0)
    ci = lax.broadcasted_iota(jnp.int32, (c, c), 1)
    causal = ri >= ci
    tri = jnp.where(causal, 1.0, 0.0).astype(BF)
    ones = jnp.ones((c, B_DV), BF)

    def chunk(ic, carry):
        rows = pl.ds(pl.multiple_of(ic * c, c), c)
        la_all = _gla_log_decay(z_ref, wa_ref, ba_ref, rows)
        for h in range(B_HEADS):
            ks = slice(h * B_DK, (h + 1) * B_DK)
            vs = slice(h * B_DV, (h + 1) * B_DV)
            la = la_all[:, ks]
            hi, mid, lo = _split3(la)
            cum = _dot(tri, hi) + _dot(tri, mid) + _dot(tri, lo)
            tot_col = _dot_tn(hi, ones) + _dot_tn(mid, ones) + _dot_tn(lo, ones)
            q = q_ref[rows, ks] * (B_DK ** -0.5)
            k = k_ref[rows, ks]
            v = v_ref[rows, vs].astype(BF)
            q_t = (q * jnp.exp(cum)).astype(BF)
            k_t = (k * jnp.exp(-cum)).astype(BF)
            scores = jnp.where(causal, _dot_nt(q_t, k_t), 0.0)
            s = s_scr[h]
            o = _dot(scores.astype(BF), v) + _dot(q_t, s.astype(BF))
            total = cum[c - 1:c, :]
            k_end = (k * jnp.exp(total - cum)).astype(BF)
            s_scr[h] = jnp.exp(tot_col) * s + _dot_tn(k_end, v)
            y_ref[rows, vs] = _gla_out(o, g_ref[:, vs], r_ref[rows, vs]).astype(BF)
        return carry

    lax.fori_loop(0, n_chunks, chunk, 0)
    so_ref[...] = s_scr[...]


def _gla_alpha(w_alpha, b_alpha):
    wa = jnp.zeros((S_WIDTH, B_HEADS * B_DK), F32).at[S_BZ:S_BZ + B_GATE_RANK].set(w_alpha).astype(BF)
    return wa, b_alpha.reshape(1, -1)


def _gla_prompt(proj, small, bsz, t, w_alpha, b_alpha, norm_g):
    tb = _tile(t, 512, B_CHUNK)
    nb = t // tb
    hk = B_HEADS * B_DK
    hv = B_HEADS * B_DV
    wa, ba = _gla_alpha(w_alpha, b_alpha)
    row = lambda b, i: b * nb + i
    y, s = _pcall(
        functools.partial(_gla_p_kernel, n_chunks=tb // B_CHUNK), grid=(bsz, nb),
        in_specs=[pl.BlockSpec((tb, hk), lambda b, i: (row(b, i), C_BQ // hk)),
                  pl.BlockSpec((tb, hk), lambda b, i: (row(b, i), C_BK // hk)),
                  pl.BlockSpec((tb, hv), lambda b, i: (row(b, i), C_BV // hv)),
                  pl.BlockSpec((tb, hv), lambda b, i: (row(b, i), C_BR // hv)),
                  pl.BlockSpec((tb, S_WIDTH), lambda b, i: (row(b, i), 0)),
                  pl.BlockSpec((S_WIDTH, hk), lambda b, i: (0, 0)),
                  pl.BlockSpec((1, hk), lambda b, i: (0, 0)),
                  pl.BlockSpec((1, hv), lambda b, i: (0, 0))],
        out_specs=[pl.BlockSpec((tb, hv), lambda b, i: (row(b, i), 0)),
                   pl.BlockSpec((None, B_HEADS, B_DK, B_DV), lambda b, i: (b, 0, 0, 0))],
        out_shape=[jax.ShapeDtypeStruct((bsz * t, hv), BF),
                   jax.ShapeDtypeStruct((bsz, B_HEADS, B_DK, B_DV), F32)],
        scratch_shapes=[pltpu.VMEM((B_HEADS, B_DK, B_DV), F32)],
        name="gla_prompt")(proj, proj, proj, proj, small, wa, ba, norm_g.reshape(1, -1))
    return y, s


def _gla_s_kernel(q_ref, k_ref, v_ref, r_ref, z_ref, wa_ref, ba_ref, g_ref, s_ref, y_ref, so_ref, *, bb):
    la_all = _gla_log_decay(z_ref, wa_ref, ba_ref, slice(None))
    ri = lax.broadcasted_iota(jnp.int32, (B_DK, B_DK), 0)
    ci = lax.broadcasted_iota(jnp.int32, (B_DK, B_DK), 1)
    eye = ri == ci
    ones = jnp.ones((B_DK, B_DV), BF)

    def diag(row):
        return jnp.where(eye, jnp.broadcast_to(row, (B_DK, B_DK)), 0.0)

    for b in range(bb):
        rs = slice(b, b + 1)
        for h in range(B_HEADS):
            ks = slice(h * B_DK, (h + 1) * B_DK)
            vs = slice(h * B_DV, (h + 1) * B_DV)
            la = la_all[rs, ks]
            k = k_ref[rs, ks]
            v = v_ref[rs, vs]
            q_t = q_ref[rs, ks] * (B_DK ** -0.5) * jnp.exp(la)
            k_t = k * jnp.exp(-la)
            score = jnp.sum(q_t * k_t, axis=-1, keepdims=True)
            s0 = s_ref[b, h]
            q8 = jnp.broadcast_to(q_t, (8, B_DK)).astype(BF)
            o = score * v + _dot(q8, s0.astype(BF))[0:1]
            hi, mid, lo = _split3(diag(la))
            la_col = _dot(hi, ones) + _dot(mid, ones) + _dot(lo, ones)
            kv = _dot(diag(k).astype(BF), jnp.broadcast_to(v, (B_DK, B_DV)).astype(BF))
            so_ref[b, h] = jnp.exp(la_col) * s0 + kv
            y_ref[rs, vs] = _gla_out(o, g_ref[:, vs], r_ref[rs, vs])


def _gla_sample(proj, small, row0, n, w_alpha, b_alpha, norm_g, state):
    bb = 8
    hk = B_HEADS * B_DK
    hv = B_HEADS * B_DV
    wa, ba = _gla_alpha(w_alpha, b_alpha)
    r0 = row0 // bb
    y, s = _pcall(
        functools.partial(_gla_s_kernel, bb=bb), grid=(n // bb,),
        in_specs=[pl.BlockSpec((bb, hk), lambda i: (r0 + i, C_BQ // hk)),
                  pl.BlockSpec((bb, hk), lambda i: (r0 + i, C_BK // hk)),
                  pl.BlockSpec((bb, hv), lambda i: (r0 + i, C_BV // hv)),
                  pl.BlockSpec((bb, hv), lambda i: (r0 + i, C_BR // hv)),
                  pl.BlockSpec((bb, S_WIDTH), lambda i: (r0 + i, 0)),
                  pl.BlockSpec((S_WIDTH, hk), lambda i: (0, 0)),
                  pl.BlockSpec((1, hk), lambda i: (0, 0)),
                  pl.BlockSpec((1, hv), lambda i: (0, 0)),
                  pl.BlockSpec((bb, B_HEADS, B_DK, B_DV), lambda i: (i, 0, 0, 0))],
        out_specs=[pl.BlockSpec((bb, hv), lambda i: (i, 0)),
                   pl.BlockSpec((bb, B_HEADS, B_DK, B_DV), lambda i: (i, 0, 0, 0))],
        out_shape=[jax.ShapeDtypeStruct((n, hv), F32),
                   jax.ShapeDtypeStruct((n, B_HEADS, B_DK, B_DV), F32)],
        name="gla_sample")(proj, proj, proj, proj, small, wa, ba, norm_g.reshape(1, -1), state)
    return y.astype(BF), s


def _conv_p_kernel(a_ref, g_ref, ah_ref, gh_ref, w_ref, cb_ref, lg_ref, lb_ref, y_ref, st_ref, buf, *, tb):
    i = pl.program_id(1)
    glu = a_ref[...] * jax.nn.sigmoid(g_ref[...])
    halo = ah_ref[...] * jax.nn.sigmoid(gh_ref[...])
    buf[0:CONV_HALO, :] = jnp.where(i > 0, halo, 0.0)
    buf[CONV_HALO:, :] = glu
    acc = jnp.broadcast_to(cb_ref[...], glu.shape)
    off = CONV_HALO - (CONV_WIDTH - 1)
    for j in range(CONV_WIDTH):
        acc = acc + w_ref[j:j + 1, :] * buf[off + j:off + j + tb, :]
    y_ref[...] = _silu(_layer_norm(acc, lg_ref[...], lb_ref[...])).astype(BF)
    st_ref[...] = glu[tb - CONV_HALO:, :]


def _conv_prompt(proj, bsz, t, conv_w, conv_b, ln_g, ln_b):
    w = BRANCH_WIDTH
    tb = _tile(t, 256, CONV_HALO)
    nb = t // tb
    per = tb // CONV_HALO
    row = lambda b, i: b * nb + i
    halo = lambda b, i: jnp.maximum((b * nb + i) * per - 1, 0)
    vec = pl.BlockSpec((1, w), lambda b, i: (0, 0))
    y, st = _pcall(
        functools.partial(_conv_p_kernel, tb=tb), grid=(bsz, nb),
        in_specs=[pl.BlockSpec((tb, w), lambda b, i: (row(b, i), C_CA // w)),
                  pl.BlockSpec((tb, w), lambda b, i: (row(b, i), C_CG // w)),
                  pl.BlockSpec((CONV_HALO, w), lambda b, i: (halo(b, i), C_CA // w)),
                  pl.BlockSpec((CONV_HALO, w), lambda b, i: (halo(b, i), C_CG // w)),
                  pl.BlockSpec((CONV_WIDTH, w), lambda b, i: (0, 0)), vec, vec, vec],
        out_specs=[pl.BlockSpec((tb, w), lambda b, i: (row(b, i), 0)),
                   pl.BlockSpec((None, CONV_HALO, w), lambda b, i: (b, 0, 0))],
        out_shape=[jax.ShapeDtypeStruct((bsz * t, w), BF), jax.ShapeDtypeStruct((bsz, CONV_HALO, w), F32)],
        scratch_shapes=[pltpu.VMEM((tb + CONV_HALO, w), F32)],
        name="conv_prompt")(proj, proj, proj, proj, conv_w, conv_b.reshape(1, w), ln_g.reshape(1, w),
                            ln_b.reshape(1, w))
    return y, st[:, CONV_HALO - (CONV_WIDTH - 1):, :]


def _conv_s_kernel(a_ref, g_ref, st_ref, w_ref, cb_ref, lg_ref, lb_ref, y_ref, so_ref):
    nbuf = CONV_WIDTH - 1
    glu = a_ref[...] * jax.nn.sigmoid(g_ref[...])
    acc = cb_ref[...] + w_ref[nbuf:nbuf + 1, :] * glu
    for j in range(nbuf):
        acc = acc + w_ref[j:j + 1, :] * st_ref[j]
    for j in range(nbuf - 1):
        so_ref[j] = st_ref[j + 1]
    so_ref[nbuf - 1] = glu
    y_ref[...] = _silu(_layer_norm(acc, lg_ref[...], lb_ref[...]))


def _conv_sample(proj, row0, n, state, conv_w, conv_b, ln_g, ln_b):
    w = BRANCH_WIDTH
    nbuf = CONV_WIDTH - 1
    bb = _tile(n, 32, 8)
    r0 = row0 // bb
    st = jnp.transpose(state, (1, 0, 2))
    vec = pl.BlockSpec((1, w), lambda i: (0, 0))
    y, so = _pcall(
        _conv_s_kernel, grid=(n // bb,),
        in_specs=[pl.BlockSpec((bb, w), lambda i: (r0 + i, C_CA // w)),
                  pl.BlockSpec((bb, w), lambda i: (r0 + i, C_CG // w)),
                  pl.BlockSpec((nbuf, bb, w), lambda i: (0, i, 0)),
                  pl.BlockSpec((CONV_WIDTH, w), lambda i: (0, 0)), vec, vec, vec],
        out_specs=[pl.BlockSpec((bb, w), lambda i: (i, 0)),
                   pl.BlockSpec((nbuf, bb, w), lambda i: (0, i, 0))],
        out_shape=[jax.ShapeDtypeStruct((n, w), F32), jax.ShapeDtypeStruct((nbuf, n, w), F32)],
        name="conv_sample")(proj, proj, st, conv_w, conv_b.reshape(1, w), ln_g.reshape(1, w), ln_b.reshape(1, w))
    return y.astype(BF), jnp.transpose(so, (1, 0, 2))


def _qkv_kernel(pq_ref, sm_ref, ct_ref, st_ref, gq_ref, gkv_ref, wn_ref, wr_ref, wrs_ref, wuk_ref,
                lat_ref, kro_ref, kvb_ref, krb_ref, ql_ref, qr_ref):
    cq = _rms(pq_ref[:, :D_Q_RANK], gq_ref[...]).astype(BF)
    ckv = _rms(pq_ref[:, D_Q_RANK:], gkv_ref[...])
    lat_ref[...] = ckv
    kvb_ref[...] = ckv.astype(BF)
    cos = ct_ref[...]
    sin = st_ref[...]
    s = sm_ref[:, :LANES]
    kr = s * cos + pltpu.roll(s, D_ROPE, 1) * sin
    kro_ref[...] = kr
    krb_ref[...] = kr.astype(BF)
    qn = _dot(cq, wn_ref[...])
    q1 = _dot(cq, wr_ref[...])
    q2 = _dot(cq, wrs_ref[...])
    for h in range(D_HEADS):
        hs = slice(h * LANES, (h + 1) * LANES)
        ql_ref[h] = _dot(qn[:, hs].astype(BF), wuk_ref[h]).astype(BF)
        qr_ref[h] = (q1[:, hs] * cos + q2[:, hs] * sin).astype(BF)


def _rope_tables(pos):
    half = D_ROPE // 2
    inv_freq = ROPE_BASE ** (-jnp.arange(half, dtype=F32) / half)
    ang = pos.astype(F32)[:, None] * inv_freq[None, :]
    cos = jnp.cos(ang)
    sin = jnp.sin(ang)
    zero = jnp.zeros((pos.shape[0], LANES - D_ROPE), F32)
    return (jnp.concatenate([cos, cos, zero], axis=1), jnp.concatenate([-sin, sin, zero], axis=1))


def _qkv_prep(proj, small, cos_t, sin_t, gq, gkv, w_uq, w_uk):
    m = proj.shape[0]
    rb = _tile(m, 416, 16)
    half = D_ROPE // 2
    wq = w_uq.reshape(D_Q_RANK, D_HEADS, D_NOPE + D_ROPE)
    wn = wq[:, :, :D_NOPE].reshape(D_Q_RANK, D_HEADS * D_NOPE).astype(BF)
    rope = wq[:, :, D_NOPE:]
    pad = jnp.zeros((D_Q_RANK, D_HEADS, LANES - D_ROPE), F32)
    wr = jnp.concatenate([rope, pad], axis=-1).reshape(D_Q_RANK, D_HEADS * LANES).astype(BF)
    wrs = jnp.concatenate([rope[..., half:], rope[..., :half], pad], axis=-1).reshape(D_Q_RANK, D_HEADS * LANES).astype(BF)
    wuk = jnp.transpose(w_uk, (1, 2, 0)).astype(BF)
    blk = D_Q_RANK + D_KV_RANK
    row = lambda w: pl.BlockSpec((rb, w), lambda i: (i, 0))
    full2 = lambda a: pl.BlockSpec(a.shape, lambda i: (0, 0))
    return _pcall(
        _qkv_kernel, grid=(m // rb,),
        in_specs=[pl.BlockSpec((rb, blk), lambda i: (i, C_DQ // blk)), row(S_WIDTH), row(LANES), row(LANES),
                  pl.BlockSpec((1, D_Q_RANK), lambda i: (0, 0)), pl.BlockSpec((1, D_KV_RANK), lambda i: (0, 0)),
                  full2(wn), full2(wr), full2(wrs), pl.BlockSpec(wuk.shape, lambda i: (0, 0, 0))],
        out_specs=[row(D_KV_RANK), row(LANES), row(D_KV_RANK), row(LANES),
                   pl.BlockSpec((D_HEADS, rb, D_KV_RANK), lambda i: (0, i, 0)),
                   pl.BlockSpec((D_HEADS, rb, LANES), lambda i: (0, i, 0))],
        out_shape=[jax.ShapeDtypeStruct((m, D_KV_RANK), F32), jax.ShapeDtypeStruct((m, LANES), F32),
                   jax.ShapeDtypeStruct((m, D_KV_RANK), BF), jax.ShapeDtypeStruct((m, LANES), BF),
                   jax.ShapeDtypeStruct((D_HEADS, m, D_KV_RANK), BF),
                   jax.ShapeDtypeStruct((D_HEADS, m, LANES), BF)],
        name="mla_qkv_prep")(proj, small, cos_t, sin_t, gq.reshape(1, -1), gkv.reshape(1, -1), wn, wr, wrs, wuk)


def _attn_p_kernel(ql_ref, qr_ref, kv_ref, kr_ref, o_ref, *, tq, t):
    i = pl.program_id(1)
    scale = (D_NOPE + D_ROPE) ** -0.5
    q_pos = i * tq + lax.broadcasted_iota(jnp.int32, (tq, t), 0)
    k_pos = lax.broadcasted_iota(jnp.int32, (tq, t), 1)
    visible = k_pos <= q_pos
    kv = kv_ref[...]
    kr = kr_ref[...]
    for h in range(D_HEADS):
        s = (_dot_nt(ql_ref[h], kv) + _dot_nt(qr_ref[h], kr)) * scale
        s = jnp.where(visible, s, NEG_INF)
        e = jnp.exp(s - jnp.max(s, axis=-1, keepdims=True))
        p = e / jnp.sum(e, axis=-1, keepdims=True)
        o_ref[h] = _dot(p.astype(BF), kv).astype(BF)


def _attn_prompt(ql, qr, kvb, krb, bsz, t):
    tq = _tile(t, 128, 16)
    nq = t // tq
    return _pcall(
        functools.partial(_attn_p_kernel, tq=tq, t=t), grid=(bsz, nq),
        in_specs=[pl.BlockSpec((D_HEADS, tq, D_KV_RANK), lambda b, i: (0, b * nq + i, 0)),
                  pl.BlockSpec((D_HEADS, tq, LANES), lambda b, i: (0, b * nq + i, 0)),
                  pl.BlockSpec((t, D_KV_RANK), lambda b, i: (b, 0)),
                  pl.BlockSpec((t, LANES), lambda b, i: (b, 0))],
        out_specs=pl.BlockSpec((D_HEADS, tq, D_KV_RANK), lambda b, i: (0, b * nq + i, 0)),
        out_shape=jax.ShapeDtypeStruct((D_HEADS, bsz * t, D_KV_RANK), BF),
        name="mla_attn_prompt")(ql, qr, kvb, krb)


def _decode_kernel(pt_ref, q_ref, qr_ref, *rest, n_pages):
    lat_refs = rest[:n_pages]
    kr_refs = rest[n_pages:2 * n_pages]
    ckv_ref, krn_ref, o_ref, m_scr, l_scr, acc_scr, kv_scr, kr_scr = rest[2 * n_pages:]
    g = pl.program_id(1)
    scale = (D_NOPE + D_ROPE) ** -0.5
    page = lat_refs[0].shape[0]

    @pl.when(g == 0)
    def _():
        m_scr[...] = jnp.full_like(m_scr, NEG_INF)
        l_scr[...] = jnp.zeros_like(l_scr)
        acc_scr[...] = jnp.zeros_like(acc_scr)

    for p in range(n_pages):
        kv_scr[p * page:(p + 1) * page, :] = lat_refs[p][...].astype(BF)
        kr_scr[p * page:(p + 1) * page, :] = kr_refs[p][...].astype(BF)
    q = q_ref[...]
    qr = qr_ref[:, :D_ROPE]
    kv = kv_scr[...]
    s = (_dot_nt(q, kv) + _dot_nt(qr, kr_scr[...])) * scale
    m_prev = m_scr[...]
    m_new = jnp.maximum(m_prev, jnp.max(s, axis=-1, keepdims=True))
    alpha = jnp.exp(m_prev - m_new)
    e = jnp.exp(s - m_new)
    l_scr[...] = alpha * l_scr[...] + jnp.sum(e, axis=-1, keepdims=True)
    acc_scr[...] = alpha * acc_scr[...] + _dot(e.astype(BF), kv)
    m_scr[...] = m_new

    @pl.when(g == pl.num_programs(1) - 1)
    def _():
        ckv = ckv_ref[...]
        s_self = (jnp.sum(q.astype(F32) * ckv, axis=-1, keepdims=True)
                  + jnp.sum(qr.astype(F32) * krn_ref[:, :D_ROPE], axis=-1, keepdims=True)) * scale
        m_prev = m_scr[...]
        m_new = jnp.maximum(m_prev, s_self)
        alpha = jnp.exp(m_prev - m_new)
        e_self = jnp.exp(s_self - m_new)
        denom = alpha * l_scr[...] + e_self
        o_ref[...] = (alpha * acc_scr[...] + e_self * ckv) / denom


def _attn_decode(q8, qr8, cache_latent, cache_k_rope, layer, page_table, ckv_new, kr_new):
    n, n_tab = page_table.shape
    page = cache_latent.shape[2]
    pp = _tile(n_tab, PAGES_PER_STEP, 1)
    lat_specs = [pl.BlockSpec((None, None, page, D_KV_RANK),
                              functools.partial(lambda b, g, pt, p: (layer, pt[b, g * pp + p], 0, 0), p=p))
                 for p in range(pp)]
    kr_specs = [pl.BlockSpec((None, None, page, D_ROPE),
                             functools.partial(lambda b, g, pt, p: (layer, pt[b, g * pp + p], 0, 0), p=p))
                for p in range(pp)]
    grid_spec = pltpu.PrefetchScalarGridSpec(
        num_scalar_prefetch=1, grid=(n, n_tab // pp),
        in_specs=[pl.BlockSpec((None, D_HEADS, D_KV_RANK), lambda b, g, pt: (b, 0, 0)),
                  pl.BlockSpec((None, D_HEADS, LANES), lambda b, g, pt: (b, 0, 0))] + lat_specs + kr_specs
        + [pl.BlockSpec((None, 1, D_KV_RANK), lambda b, g, pt: (b, 0, 0)),
           pl.BlockSpec((None, 1, LANES), lambda b, g, pt: (b, 0, 0))],
        out_specs=pl.BlockSpec((None, D_HEADS, D_KV_RANK), lambda b, g, pt: (b, 0, 0)),
        scratch_shapes=[pltpu.VMEM((D_HEADS, 1), F32), pltpu.VMEM((D_HEADS, 1), F32),
                        pltpu.VMEM((D_HEADS, D_KV_RANK), F32),
                        pltpu.VMEM((pp * page, D_KV_RANK), BF), pltpu.VMEM((pp * page, D_ROPE), BF)])
    return _pcall(
        functools.partial(_decode_kernel, n_pages=pp), grid=(n, n_tab // pp), in_specs=None, out_specs=None,
        out_shape=jax.ShapeDtypeStruct((n, D_HEADS, D_KV_RANK), F32), grid_spec=grid_spec,
        name="mla_decode")(page_table, q8, qr8, *([cache_latent] * pp), *([cache_k_rope] * pp),
                           ckv_new.reshape(n, 1, D_KV_RANK), kr_new.reshape(n, 1, LANES))


def _uv_kernel(o_ref, w_ref, y_ref):
    for h in range(D_HEADS):
        y_ref[:, h * D_VDIM:(h + 1) * D_VDIM] = _dot(o_ref[h], w_ref[h]).astype(BF)


def _uv_proj(olat, w_uv):
    m = olat.shape[1]
    rb = _tile(m, 512, 16)
    wuv = jnp.transpose(w_uv, (1, 0, 2)).astype(BF)
    return _pcall(
        _uv_kernel, grid=(m // rb,),
        in_specs=[pl.BlockSpec((D_HEADS, rb, D_KV_RANK), lambda i: (0, i, 0)),
                  pl.BlockSpec(wuv.shape, lambda i: (0, 0, 0))],
        out_specs=pl.BlockSpec((rb, D_HEADS * D_VDIM), lambda i: (i, 0)),
        out_shape=jax.ShapeDtypeStruct((m, D_HEADS * D_VDIM), BF), name="mla_uv")(olat, wuv)


def _split_w_in(w_in, d):
    o_bz = 4096
    o_br = o_bz + B_GATE_RANK
    o_kr = o_br + 3 * BRANCH_WIDTH + D_Q_RANK + D_KV_RANK
    o_gate = o_kr + D_ROPE
    half = D_ROPE // 2
    main = jnp.concatenate([w_in[:, :o_bz], w_in[:, o_br:o_kr], w_in[:, o_gate:]], axis=1).astype(BF)
    small = jnp.concatenate(
        [w_in[:, o_kr:o_gate], w_in[:, o_kr + half:o_gate], w_in[:, o_kr:o_kr + half], w_in[:, o_bz:o_br],
         jnp.zeros((d, S_WIDTH - 2 * D_ROPE - B_GATE_RANK), F32)], axis=1).astype(BF)
    return main, small


def kernel(x_prompt, x_sample, cache_latent, cache_k_rope, page_table, state_gla, state_conv, ffn1_norm_pre, ffn1_norm_post, ffn1_w_gu, ffn1_w_down, mix_norm_pre, mix_norm_post, w_in, a_ln_g, a_ln_b, a_w_s, a_b_s, b_w_alpha, b_b_alpha, b_norm_g, c_conv_w, c_conv_b, c_ln_g, c_ln_b, d_q_norm_g, d_kv_norm_g, d_w_uq, d_w_uk, d_w_uv, w_branch, w_out, ffn2_norm_pre, ffn2_norm_post, ffn2_w_gu, ffn2_w_down):
    bsz, t, d = x_prompt.shape
    n_s, t_s, _ = x_sample.shape
    depth = w_in.shape[0]
    assert t_s == 1 and n_s == A_CHUNK and t % A_CHUNK == 0
    mp = bsz * t
    past_len = page_table.shape[1] * cache_latent.shape[2]

    pos = jnp.concatenate([jnp.tile(jnp.arange(t, dtype=jnp.int32), bsz),
                           jnp.full((n_s,), past_len, jnp.int32)])
    cos_t, sin_t = _rope_tables(pos)

    h = jnp.concatenate([x_prompt.reshape(mp, d), x_sample.reshape(n_s, d)], axis=0)
    n = _norm_cast(h, ffn1_norm_pre[0])
    outs = [[] for _ in range(9)]
    for l in range(depth):
        hid = _mm_gu(n, ffn1_w_gu, l)
        f = _mm(hid, ffn1_w_down, l, F32, "ffn_down", single_buffer_x=True)
        h, n = _post(f, h, ffn1_norm_post[l], mix_norm_pre[l], 0.5)

        w_main, w_small = _split_w_in(w_in[l], d)
        proj = _mm(n, w_main, None, F32, "in_proj")
        small = _mm(n, w_small, None, F32, "in_proj_small")

        y_a, v_s = _mixer_a(proj, mp // A_CHUNK, a_ln_g[l], a_ln_b[l], a_w_s[l], a_b_s[l])

        yb_p, gla_p = _gla_prompt(proj, small, bsz, t, b_w_alpha[l], b_b_alpha[l], b_norm_g[l])
        yb_s, gla_s = _gla_sample(proj, small, mp, n_s, b_w_alpha[l], b_b_alpha[l], b_norm_g[l], state_gla[l])
        y_b = jnp.concatenate([yb_p, yb_s], axis=0)

        yc_p, conv_p = _conv_prompt(proj, bsz, t, c_conv_w[l], c_conv_b[l], c_ln_g[l], c_ln_b[l])
        yc_s, conv_s = _conv_sample(proj, mp, n_s, state_conv[l], c_conv_w[l], c_conv_b[l], c_ln_g[l], c_ln_b[l])
        y_c = jnp.concatenate([yc_p, yc_s], axis=0)

        lat, kro, kvb, krb, ql, qr = _qkv_prep(proj, small, cos_t, sin_t, d_q_norm_g[l], d_kv_norm_g[l],
                                               d_w_uq[l], d_w_uk[l])
        o_p = _attn_prompt(ql, qr, kvb, krb, bsz, t)
        o_s = _attn_decode(jnp.transpose(ql[:, mp:], (1, 0, 2)), jnp.transpose(qr[:, mp:], (1, 0, 2)),
                           cache_latent, cache_k_rope, l, page_table, lat[mp:], kro[mp:])
        olat = jnp.concatenate([o_p, jnp.transpose(o_s, (1, 0, 2)).astype(BF)], axis=1)
        y_d = _uv_proj(olat, d_w_uv[l])

        merged = _merge((y_a, y_b, y_c, y_d), w_branch, l, proj, d)
        f = _mm(merged, w_out, l, F32, "out_proj")
        h, n = _post(f, h, mix_norm_post[l], ffn2_norm_pre[l], 1.0)

        hid = _mm_gu(n, ffn2_w_gu, l)
        f = _mm(hid, ffn2_w_down, l, F32, "ffn_down", single_buffer_x=True)
        h, n = _post(f, h, ffn2_norm_post[l], ffn1_norm_pre[(l + 1) % depth], 0.5)

        for acc, val in zip(outs, (lat[:mp].reshape(bsz, t, D_KV_RANK), lat[mp:].reshape(n_s, 1, D_KV_RANK),
                                   kro[:mp, :D_ROPE].reshape(bsz, t, D_ROPE), kro[mp:, :D_ROPE].reshape(n_s, 1, D_ROPE),
                                   gla_p, gla_s, conv_p, conv_s, v_s.reshape(n_s, 1, BRANCH_WIDTH))):
            acc.append(val)

    return (h[:mp].reshape(bsz, t, d), h[mp:].reshape(n_s, 1, d)) + tuple(jnp.stack(o) for o in outs)
```

```python
import functools

import jax
import jax.numpy as jnp
from jax import lax
from jax.experimental import pallas as pl
from jax.experimental.pallas import tpu as pltpu

F32 = jnp.float32
BF = jnp.bfloat16

EPS = 1e-6
N_BRANCH = 4
BRANCH_WIDTH = 1024
A_GROUPS = 4
A_CHUNK = 128
B_HEADS = 4
B_DK = 128
B_DV = 256
B_GATE_RANK = 16
B_GATE_TAU = 16.0
B_CHUNK = 64
CONV_WIDTH = 31
D_HEADS = 8
D_NOPE = 128
D_ROPE = 64
D_VDIM = 128
D_Q_RANK = 768
D_KV_RANK = 256
ROPE_BASE = 10000.0
NEG_INF = -1e30

LANES = 128
VMEM_LIMIT = 56 * 1024 * 1024
PAGES_PER_STEP = 32
CONV_HALO = 32

C_AU, C_AV, C_BQ, C_BK, C_BV, C_BR, C_CA, C_CG, C_DQ, C_DKV, C_GATE = (
    0, 1024, 2048, 2560, 3072, 4096, 5120, 6144, 7168, 7936, 8192)
S_KR, S_KRS, S_BZ, S_WIDTH = 0, 64, 128, 256


def _tile(n, target, mult):
    best = None
    for t in range(mult, min(n, target) + 1, mult):
        if n % t == 0:
            best = t
    assert best is not None, (n, target, mult)
    return best


def _dot(a, b):
    return jnp.dot(a, b, preferred_element_type=F32)


def _dot_nt(a, b):
    return lax.dot_general(a, b, (((1,), (1,)), ((), ())), preferred_element_type=F32)


def _dot_tn(a, b):
    return lax.dot_general(a, b, (((0,), (0,)), ((), ())), preferred_element_type=F32)


def _split3(x):
    hi = x.astype(BF)
    r1 = x - hi.astype(F32)
    mid = r1.astype(BF)
    lo = (r1 - mid.astype(F32)).astype(BF)
    return hi, mid, lo


def _rms(x, g):
    return x * lax.rsqrt(jnp.mean(x * x, axis=-1, keepdims=True) + EPS) * g


def _layer_norm(x, g, b):
    mu = jnp.mean(x, axis=-1, keepdims=True)
    xc = x - mu
    var = jnp.mean(xc * xc, axis=-1, keepdims=True)
    return xc * lax.rsqrt(var + EPS) * g + b


def _silu(x):
    return x * jax.nn.sigmoid(x)


def _gelu(x):
    return jax.nn.gelu(x, approximate=True)


def _log_sigmoid(x):
    return jnp.minimum(x, 0.0) - jnp.log1p(jnp.exp(-jnp.abs(x)))


def _pcall(kernel, *, grid, in_specs, out_specs, out_shape, scratch_shapes=(), name=None, grid_spec=None):
    params = pltpu.CompilerParams(dimension_semantics=("arbitrary",) * len(grid), vmem_limit_bytes=VMEM_LIMIT)
    if grid_spec is not None:
        return pl.pallas_call(kernel, grid_spec=grid_spec, out_shape=out_shape, compiler_params=params, name=name)
    return pl.pallas_call(kernel, grid=grid, in_specs=in_specs, out_specs=out_specs, out_shape=out_shape,
                          scratch_shapes=scratch_shapes, compiler_params=params, name=name)


def _norm_kernel(x_ref, g_ref, o_ref):
    o_ref[...] = _rms(x_ref[...], g_ref[...]).astype(BF)


def _norm_cast(h, g):
    m, d = h.shape
    rb = _tile(m, 416, 16)
    return _pcall(
        _norm_kernel, grid=(m // rb,),
        in_specs=[pl.BlockSpec((rb, d), lambda i: (i, 0)), pl.BlockSpec((1, d), lambda i: (0, 0))],
        out_specs=pl.BlockSpec((rb, d), lambda i: (i, 0)),
        out_shape=jax.ShapeDtypeStruct((m, d), BF), name="norm_cast")(h, g.reshape(1, d))


def _post_kernel(f_ref, h_ref, gp_ref, gn_ref, ho_ref, no_ref, *, scale):
    h = h_ref[...] + scale * _rms(f_ref[...], gp_ref[...])
    ho_ref[...] = h
    no_ref[...] = _rms(h, gn_ref[...]).astype(BF)


def _post(f, h, g_post, g_next, scale):
    m, d = h.shape
    rb = _tile(m, 208, 16)
    row = pl.BlockSpec((rb, d), lambda i: (i, 0))
    vec = pl.BlockSpec((1, d), lambda i: (0, 0))
    return _pcall(
        functools.partial(_post_kernel, scale=scale), grid=(m // rb,),
        in_specs=[row, row, vec, vec], out_specs=[row, row],
        out_shape=[jax.ShapeDtypeStruct((m, d), F32), jax.ShapeDtypeStruct((m, d), BF)],
        name="post_norm")(f, h, g_post.reshape(1, d), g_next.reshape(1, d))


def _mm_kernel(x_ref, w_ref, o_ref):
    o_ref[...] = _dot(x_ref[...], w_ref[...].astype(BF)).astype(o_ref.dtype)


def _mm(x, w, layer, out_dtype, name, single_buffer_x=False):
    m, k = x.shape
    n = w.shape[-1]
    tm = _tile(m, 832, 16)
    tn = _tile(n, 512, 2 * LANES) if n % (2 * LANES) == 0 else _tile(n, 512, LANES)
    if k > 8192:
        tn = _tile(n, 256, LANES)
    if layer is None:
        w_spec = pl.BlockSpec((k, tn), lambda i, j: (0, j))
    else:
        w_spec = pl.BlockSpec((None, k, tn), lambda i, j: (layer, 0, j))
    if single_buffer_x:
        x_spec = pl.BlockSpec((tm, k), lambda i, j: (i, 0), pipeline_mode=pl.Buffered(1))
    else:
        x_spec = pl.BlockSpec((tm, k), lambda i, j: (i, 0))
    return _pcall(
        _mm_kernel, grid=(m // tm, n // tn), in_specs=[x_spec, w_spec],
        out_specs=pl.BlockSpec((tm, tn), lambda i, j: (i, j)),
        out_shape=jax.ShapeDtypeStruct((m, n), out_dtype), name=name)(x, w)


def _mm_nt_kernel(x_ref, w_ref, o_ref):
    o_ref[...] = _dot_nt(x_ref[...], w_ref[...].astype(BF)).astype(o_ref.dtype)


W_IN_BZ = 4096
W_IN_BR = W_IN_BZ + B_GATE_RANK
W_IN_KR = W_IN_BR + 3 * BRANCH_WIDTH + D_Q_RANK + D_KV_RANK
W_IN_GATE = W_IN_KR + D_ROPE


def _in_proj(x, w_t, layer, d):
    m, k = x.shape
    tm = _tile(m, 832, 16)
    tn = 512
    run0 = W_IN_BZ // tn
    run1 = run0 + (W_IN_KR - W_IN_BR) // tn
    n_out = W_IN_BZ + (W_IN_KR - W_IN_BR) + N_BRANCH * d
    assert W_IN_BZ % tn == 0 and (W_IN_KR - W_IN_BR) % tn == 0 and (N_BRANCH * d) % tn == 0

    sub = 8
    assert W_IN_BR % sub == 0 and W_IN_GATE % sub == 0

    def w_row(j):
        return sub * jnp.where(j < run0, j * (tn // sub),
                               jnp.where(j < run1, W_IN_BR // sub + (j - run0) * (tn // sub),
                                         W_IN_GATE // sub + (j - run1) * (tn // sub)))

    def nt_kernel(x_ref, w_ref, o_ref):
        _mm_nt_kernel(x_ref, w_ref.at[0], o_ref)

    return _pcall(
        nt_kernel, grid=(m // tm, n_out // tn),
        in_specs=[pl.BlockSpec((tm, k), lambda i, j: (i, 0)),
                  pl.BlockSpec((pl.Element(1), pl.Element(tn), pl.Element(k)),
                               lambda i, j: (layer, w_row(j), 0))],
        out_specs=pl.BlockSpec((tm, tn), lambda i, j: (i, j)),
        out_shape=jax.ShapeDtypeStruct((m, n_out), F32), name="in_proj")(x, w_t)


def _in_proj_small(x, w_t, layer):
    m, k = x.shape
    half = D_ROPE // 2
    w = w_t[layer]
    ws = jnp.concatenate(
        [w[W_IN_KR:W_IN_GATE], w[W_IN_KR + half:W_IN_GATE], w[W_IN_KR:W_IN_KR + half], w[W_IN_BZ:W_IN_BR],
         jnp.zeros((S_WIDTH - 2 * D_ROPE - B_GATE_RANK, k), F32)], axis=0).astype(BF)
    tm = _tile(m, 832, 16)
    return _pcall(
        _mm_nt_kernel, grid=(m // tm,),
        in_specs=[pl.BlockSpec((tm, k), lambda i: (i, 0)), pl.BlockSpec((S_WIDTH, k), lambda i: (0, 0))],
        out_specs=pl.BlockSpec((tm, S_WIDTH), lambda i: (i, 0)),
        out_shape=jax.ShapeDtypeStruct((m, S_WIDTH), F32), name="in_proj_small")(x, ws)


def _mm_gu_kernel(x_ref, wg_ref, wu_ref, o_ref):
    x = x_ref[...]
    a = _dot(x, wg_ref[...].astype(BF))
    b = _dot(x, wu_ref[...].astype(BF))
    o_ref[...] = (_silu(a) * b).astype(BF)


def _mm_gu(x, w_gu, layer):
    m, k = x.shape
    f = w_gu.shape[-1] // 2
    tm = _tile(m, 832, 16)
    tn = _tile(f, 256, LANES)
    nj = f // tn
    return _pcall(
        _mm_gu_kernel, grid=(m // tm, nj),
        in_specs=[pl.BlockSpec((tm, k), lambda i, j: (i, 0)),
                  pl.BlockSpec((None, k, tn), lambda i, j: (layer, 0, j)),
                  pl.BlockSpec((None, k, tn), lambda i, j: (layer, 0, j + nj))],
        out_specs=pl.BlockSpec((tm, tn), lambda i, j: (i, j)),
        out_shape=jax.ShapeDtypeStruct((m, f), BF), name="ffn_gate_up")(x, w_gu, w_gu)


def _merge_kernel(ya_ref, yb_ref, yc_ref, yd_ref, w_ref, g0_ref, g1_ref, g2_ref, g3_ref, o_ref):
    acc = None
    for b, (y_ref, g_ref) in enumerate(zip((ya_ref, yb_ref, yc_ref, yd_ref), (g0_ref, g1_ref, g2_ref, g3_ref))):
        t = _dot(y_ref[...], w_ref[b].astype(BF)) * jax.nn.sigmoid(g_ref[...])
        acc = t if acc is None else acc + t
    o_ref[...] = acc.astype(BF)


def _merge(ys, w_branch, layer, proj, d):
    m = proj.shape[0]
    tm = _tile(m, 832, 16)
    tn = _tile(d, 512, LANES)
    y_spec = pl.BlockSpec((tm, BRANCH_WIDTH), lambda i, j: (i, 0))
    gate_specs = [pl.BlockSpec((tm, tn), functools.partial(lambda i, j, b: (i, (C_GATE + b * d) // tn + j), b=b))
                  for b in range(N_BRANCH)]
    return _pcall(
        _merge_kernel, grid=(m // tm, d // tn),
        in_specs=[y_spec] * N_BRANCH
        + [pl.BlockSpec((None, N_BRANCH, BRANCH_WIDTH, tn), lambda i, j: (layer, 0, 0, j))] + gate_specs,
        out_specs=pl.BlockSpec((tm, tn), lambda i, j: (i, j)),
        out_shape=jax.ShapeDtypeStruct((m, d), BF), name="gated_merge")(*ys, w_branch, proj, proj, proj, proj)


def _a_kernel(p_ref, lg_ref, lb_ref, ws_ref, bs_ref, y_ref, v_ref):
    u = _gelu(p_ref[:, :BRANCH_WIDTH])
    v = _layer_norm(_gelu(p_ref[:, BRANCH_WIDTH:]), lg_ref[...], lb_ref[...])
    v_ref[...] = v
    vb = v.astype(BF)
    gw = BRANCH_WIDTH // A_GROUPS
    for g in range(A_GROUPS):
        cols = slice(g * gw, (g + 1) * gw)
        mixed = _dot(ws_ref[g], vb[:, cols]) + bs_ref[g]
        y_ref[:, cols] = (u[:, cols] * mixed).astype(BF)


def _mixer_a(proj, n_prompt_chunks, ln_g, ln_b, w_s, b_s):
    m = proj.shape[0]
    c = A_CHUNK
    causal = jnp.tril(jnp.ones((c, c), dtype=bool))
    ws = jnp.stack([jnp.where(causal, w_s, 0.0),
                    w_s[:, :1, :1] * jnp.eye(c, dtype=F32)[None]]).astype(BF)
    bs = jnp.stack([b_s, jnp.broadcast_to(b_s[:, :1], b_s.shape)])[..., None]
    kind = lambda i: i // n_prompt_chunks
    y, v = _pcall(
        _a_kernel, grid=(m // c,),
        in_specs=[pl.BlockSpec((c, 2 * BRANCH_WIDTH), lambda i: (i, C_AU // (2 * BRANCH_WIDTH))),
                  pl.BlockSpec((1, BRANCH_WIDTH), lambda i: (0, 0)),
                  pl.BlockSpec((1, BRANCH_WIDTH), lambda i: (0, 0)),
                  pl.BlockSpec((None, A_GROUPS, c, c), lambda i: (kind(i), 0, 0, 0)),
                  pl.BlockSpec((None, A_GROUPS, c, 1), lambda i: (kind(i), 0, 0, 0))],
        out_specs=[pl.BlockSpec((c, BRANCH_WIDTH), lambda i: (i, 0)),
                   pl.BlockSpec((c, BRANCH_WIDTH), lambda i: (0, 0))],
        out_shape=[jax.ShapeDtypeStruct((m, BRANCH_WIDTH), BF), jax.ShapeDtypeStruct((c, BRANCH_WIDTH), F32)],
        name="mixer_a")(proj, ln_g.reshape(1, -1), ln_b.reshape(1, -1), ws, bs)
    return y, v


def _gla_log_decay(z_ref, wa_ref, ba_ref, rows):
    z = z_ref[rows, :].astype(BF)
    return _log_sigmoid(_dot(z, wa_ref[...]) + ba_ref[...]) * (1.0 / B_GATE_TAU)


def _gla_out(o, g, r):
    return _silu(r) * _rms(o, g)


def _gla_p_kernel(q_ref, k_ref, v_ref, r_ref, z_ref, wa_ref, ba_ref, g_ref, y_ref, so_ref, s_scr, *, n_chunks):
    blk = pl.program_id(1)
    c = B_CHUNK

    @pl.when(blk == 0)
    def _():
        s_scr[...] = jnp.zeros_like(s_scr)

    ri = lax.broadcasted_iota(jnp.int32, (c, c), 0)
    ci = lax.broadcasted_iota(jnp.int32, (c, c), 1)
    causal = ri >= ci
    tri = jnp.where(causal, 1.0, 0.0).astype(BF)
    ones = jnp.ones((c, B_DV), BF)

    def chunk(ic, carry):
        rows = pl.ds(pl.multiple_of(ic * c, c), c)
        la_all = _gla_log_decay(z_ref, wa_ref, ba_ref, rows)
        for h in range(B_HEADS):
            ks = slice(h * B_DK, (h + 1) * B_DK)
            vs = slice(h * B_DV, (h + 1) * B_DV)
            la = la_all[:, ks]
            hi, mid, lo = _split3(la)
            cum = _dot(tri, hi) + _dot(tri, mid) + _dot(tri, lo)
            tot_col = _dot_tn(hi, ones) + _dot_tn(mid, ones) + _dot_tn(lo, ones)
            q = q_ref[rows, ks] * (B_DK ** -0.5)
            k = k_ref[rows, ks]
            v = v_ref[rows, vs].astype(BF)
            q_t = (q * jnp.exp(cum)).astype(BF)
            k_t = (k * jnp.exp(-cum)).astype(BF)
            scores = jnp.where(causal, _dot_nt(q_t, k_t), 0.0)
            s = s_scr[h]
            o = _dot(scores.astype(BF), v) + _dot(q_t, s.astype(BF))
            total = cum[c - 1:c, :]
            k_end = (k * jnp.exp(total - cum)).astype(BF)
            s_scr[h] = jnp.exp(tot_col) * s + _dot_tn(k_end, v)
            y_ref[rows, vs] = _gla_out(o, g_ref[:, vs], r_ref[rows, vs]).astype(BF)
        return carry

    lax.fori_loop(0, n_chunks, chunk, 0)
    so_ref[...] = s_scr[...]


def _gla_alpha(w_alpha, b_alpha):
    wa = jnp.zeros((S_WIDTH, B_HEADS * B_DK), F32).at[S_BZ:S_BZ + B_GATE_RANK].set(w_alpha).astype(BF)
    return wa, b_alpha.reshape(1, -1)


def _gla_prompt(proj, small, bsz, t, w_alpha, b_alpha, norm_g):
    tb = _tile(t, 512, B_CHUNK)
    nb = t // tb
    hk = B_HEADS * B_DK
    hv = B_HEADS * B_DV
    wa, ba = _gla_alpha(w_alpha, b_alpha)
    row = lambda b, i: b * nb + i
    y, s = _pcall(
        functools.partial(_gla_p_kernel, n_chunks=tb // B_CHUNK), grid=(bsz, nb),
        in_specs=[pl.BlockSpec((tb, hk), lambda b, i: (row(b, i), C_BQ // hk)),
                  pl.BlockSpec((tb, hk), lambda b, i: (row(b, i), C_BK // hk)),
                  pl.BlockSpec((tb, hv), lambda b, i: (row(b, i), C_BV // hv)),
                  pl.BlockSpec((tb, hv), lambda b, i: (row(b, i), C_BR // hv)),
                  pl.BlockSpec((tb, S_WIDTH), lambda b, i: (row(b, i), 0)),
                  pl.BlockSpec((S_WIDTH, hk), lambda b, i: (0, 0)),
                  pl.BlockSpec((1, hk), lambda b, i: (0, 0)),
                  pl.BlockSpec((1, hv), lambda b, i: (0, 0))],
        out_specs=[pl.BlockSpec((tb, hv), lambda b, i: (row(b, i), 0)),
                   pl.BlockSpec((None, B_HEADS, B_DK, B_DV), lambda b, i: (b, 0, 0, 0))],
        out_shape=[jax.ShapeDtypeStruct((bsz * t, hv), BF),
                   jax.ShapeDtypeStruct((bsz, B_HEADS, B_DK, B_DV), F32)],
        scratch_shapes=[pltpu.VMEM((B_HEADS, B_DK, B_DV), F32)],
        name="gla_prompt")(proj, proj, proj, proj, small, wa, ba, norm_g.reshape(1, -1))
    return y, s


def _gla_s_kernel(q_ref, k_ref, v_ref, r_ref, z_ref, wa_ref, ba_ref, g_ref, s_ref, y_ref, so_ref, *, bb):
    la_all = _gla_log_decay(z_ref, wa_ref, ba_ref, slice(None))
    ri = lax.broadcasted_iota(jnp.int32, (B_DK, B_DK), 0)
    ci = lax.broadcasted_iota(jnp.int32, (B_DK, B_DK), 1)
    eye = ri == ci
    ones = jnp.ones((B_DK, B_DV), BF)

    def diag(row):
        return jnp.where(eye, jnp.broadcast_to(row, (B_DK, B_DK)), 0.0)

    for b in range(bb):
        rs = slice(b, b + 1)
        for h in range(B_HEADS):
            ks = slice(h * B_DK, (h + 1) * B_DK)
            vs = slice(h * B_DV, (h + 1) * B_DV)
            la = la_all[rs, ks]
            k = k_ref[rs, ks]
            v = v_ref[rs, vs]
            q_t = q_ref[rs, ks] * (B_DK ** -0.5) * jnp.exp(la)
            k_t = k * jnp.exp(-la)
            score = jnp.sum(q_t * k_t, axis=-1, keepdims=True)
            s0 = s_ref[b, h]
            q8 = jnp.broadcast_to(q_t, (8, B_DK)).astype(BF)
            o = score * v + _dot(q8, s0.astype(BF))[0:1]
            hi, mid, lo = _split3(diag(la))
            la_col = _dot(hi, ones) + _dot(mid, ones) + _dot(lo, ones)
            kv = _dot(diag(k).astype(BF), jnp.broadcast_to(v, (B_DK, B_DV)).astype(BF))
            so_ref[b, h] = jnp.exp(la_col) * s0 + kv
            y_ref[rs, vs] = _gla_out(o, g_ref[:, vs], r_ref[rs, vs])


def _gla_sample(proj, small, row0, n, w_alpha, b_alpha, norm_g, state, layer):
    bb = 8
    hk = B_HEADS * B_DK
    hv = B_HEADS * B_DV
    wa, ba = _gla_alpha(w_alpha, b_alpha)
    r0 = row0 // bb
    y, s = _pcall(
        functools.partial(_gla_s_kernel, bb=bb), grid=(n // bb,),
        in_specs=[pl.BlockSpec((bb, hk), lambda i: (r0 + i, C_BQ // hk)),
                  pl.BlockSpec((bb, hk), lambda i: (r0 + i, C_BK // hk)),
                  pl.BlockSpec((bb, hv), lambda i: (r0 + i, C_BV // hv)),
                  pl.BlockSpec((bb, hv), lambda i: (r0 + i, C_BR // hv)),
                  pl.BlockSpec((bb, S_WIDTH), lambda i: (r0 + i, 0)),
                  pl.BlockSpec((S_WIDTH, hk), lambda i: (0, 0)),
                  pl.BlockSpec((1, hk), lambda i: (0, 0)),
                  pl.BlockSpec((1, hv), lambda i: (0, 0)),
                  pl.BlockSpec((None, bb, B_HEADS, B_DK, B_DV), lambda i: (layer, i, 0, 0, 0))],
        out_specs=[pl.BlockSpec((bb, hv), lambda i: (i, 0)),
                   pl.BlockSpec((bb, B_HEADS, B_DK, B_DV), lambda i: (i, 0, 0, 0))],
        out_shape=[jax.ShapeDtypeStruct((n, hv), F32),
                   jax.ShapeDtypeStruct((n, B_HEADS, B_DK, B_DV), F32)],
        name="gla_sample")(proj, proj, proj, proj, small, wa, ba, norm_g.reshape(1, -1), state)
    return y.astype(BF), s


def _conv_p_kernel(a_ref, g_ref, ah_ref, gh_ref, w_ref, cb_ref, lg_ref, lb_ref, y_ref, st_ref, buf, *, tb):
    i = pl.program_id(1)
    glu = a_ref[...] * jax.nn.sigmoid(g_ref[...])
    halo = ah_ref[...] * jax.nn.sigmoid(gh_ref[...])
    buf[0:CONV_HALO, :] = jnp.where(i > 0, halo, 0.0)
    buf[CONV_HALO:, :] = glu
    acc = jnp.broadcast_to(cb_ref[...], glu.shape)
    off = CONV_HALO - (CONV_WIDTH - 1)
    for j in range(CONV_WIDTH):
        acc = acc + w_ref[j:j + 1, :] * buf[off + j:off + j + tb, :]
    y_ref[...] = _silu(_layer_norm(acc, lg_ref[...], lb_ref[...])).astype(BF)
    st_ref[...] = glu[tb - CONV_HALO:, :]


def _conv_prompt(proj, bsz, t, conv_w, conv_b, ln_g, ln_b):
    w = BRANCH_WIDTH
    tb = _tile(t, 256, CONV_HALO)
    nb = t // tb
    per = tb // CONV_HALO
    row = lambda b, i: b * nb + i
    halo = lambda b, i: jnp.maximum((b * nb + i) * per - 1, 0)
    vec = pl.BlockSpec((1, w), lambda b, i: (0, 0))
    y, st = _pcall(
        functools.partial(_conv_p_kernel, tb=tb), grid=(bsz, nb),
        in_specs=[pl.BlockSpec((tb, w), lambda b, i: (row(b, i), C_CA // w)),
                  pl.BlockSpec((tb, w), lambda b, i: (row(b, i), C_CG // w)),
                  pl.BlockSpec((CONV_HALO, w), lambda b, i: (halo(b, i), C_CA // w)),
                  pl.BlockSpec((CONV_HALO, w), lambda b, i: (halo(b, i), C_CG // w)),
                  pl.BlockSpec((CONV_WIDTH, w), lambda b, i: (0, 0)), vec, vec, vec],
        out_specs=[pl.BlockSpec((tb, w), lambda b, i: (row(b, i), 0)),
                   pl.BlockSpec((None, CONV_HALO, w), lambda b, i: (b, 0, 0))],
        out_shape=[jax.ShapeDtypeStruct((bsz * t, w), BF), jax.ShapeDtypeStruct((bsz, CONV_HALO, w), F32)],
        scratch_shapes=[pltpu.VMEM((tb + CONV_HALO, w), F32)],
        name="conv_prompt")(proj, proj, proj, proj, conv_w, conv_b.reshape(1, w), ln_g.reshape(1, w),
                            ln_b.reshape(1, w))
    return y, st[:, CONV_HALO - (CONV_WIDTH - 1):, :]


def _conv_s_kernel(a_ref, g_ref, st_ref, w_ref, cb_ref, lg_ref, lb_ref, y_ref, so_ref):
    nbuf = CONV_WIDTH - 1
    glu = a_ref[...] * jax.nn.sigmoid(g_ref[...])
    acc = cb_ref[...] + w_ref[nbuf:nbuf + 1, :] * glu
    for j in range(nbuf):
        acc = acc + w_ref[j:j + 1, :] * st_ref[j]
    for j in range(nbuf - 1):
        so_ref[j] = st_ref[j + 1]
    so_ref[nbuf - 1] = glu
    y_ref[...] = _silu(_layer_norm(acc, lg_ref[...], lb_ref[...]))


def _conv_sample(proj, row0, n, state, layer, conv_w, conv_b, ln_g, ln_b):
    w = BRANCH_WIDTH
    nbuf = CONV_WIDTH - 1
    bb = _tile(n, 32, 8)
    r0 = row0 // bb
    st = jnp.transpose(state, (0, 2, 1, 3))
    vec = pl.BlockSpec((1, w), lambda i: (0, 0))
    y, so = _pcall(
        _conv_s_kernel, grid=(n // bb,),
        in_specs=[pl.BlockSpec((bb, w), lambda i: (r0 + i, C_CA // w)),
                  pl.BlockSpec((bb, w), lambda i: (r0 + i, C_CG // w)),
                  pl.BlockSpec((None, nbuf, bb, w), lambda i: (layer, 0, i, 0)),
                  pl.BlockSpec((CONV_WIDTH, w), lambda i: (0, 0)), vec, vec, vec],
        out_specs=[pl.BlockSpec((bb, w), lambda i: (i, 0)),
                   pl.BlockSpec((nbuf, bb, w), lambda i: (0, i, 0))],
        out_shape=[jax.ShapeDtypeStruct((n, w), F32), jax.ShapeDtypeStruct((nbuf, n, w), F32)],
        name="conv_sample")(proj, proj, st, conv_w, conv_b.reshape(1, w), ln_g.reshape(1, w), ln_b.reshape(1, w))
    return y.astype(BF), jnp.transpose(so, (1, 0, 2))


def _qkv_kernel(pq_ref, sm_ref, ct_ref, st_ref, gq_ref, gkv_ref, wn_ref, wr_ref, wrs_ref, wuk_ref,
                lat_ref, kro_ref, kvb_ref, krb_ref, ql_ref, qr_ref):
    cq = _rms(pq_ref[:, :D_Q_RANK], gq_ref[...]).astype(BF)
    ckv = _rms(pq_ref[:, D_Q_RANK:], gkv_ref[...])
    lat_ref[...] = ckv
    kvb_ref[...] = ckv.astype(BF)
    cos = ct_ref[...]
    sin = st_ref[...]
    s = sm_ref[:, :LANES]
    kr = s * cos + pltpu.roll(s, D_ROPE, 1) * sin
    kro_ref[...] = kr
    krb_ref[...] = kr.astype(BF)
    qn = _dot(cq, wn_ref[...])
    q1 = _dot(cq, wr_ref[...])
    q2 = _dot(cq, wrs_ref[...])
    for h in range(D_HEADS):
        hs = slice(h * LANES, (h + 1) * LANES)
        ql_ref[h] = _dot(qn[:, hs].astype(BF), wuk_ref[h]).astype(BF)
        qr_ref[h] = (q1[:, hs] * cos + q2[:, hs] * sin).astype(BF)


def _rope_tables(pos):
    half = D_ROPE // 2
    inv_freq = ROPE_BASE ** (-jnp.arange(half, dtype=F32) / half)
    ang = pos.astype(F32)[:, None] * inv_freq[None, :]
    cos = jnp.cos(ang)
    sin = jnp.sin(ang)
    zero = jnp.zeros((pos.shape[0], LANES - D_ROPE), F32)
    return (jnp.concatenate([cos, cos, zero], axis=1), jnp.concatenate([-sin, sin, zero], axis=1))


def _qkv_prep(proj, small, cos_t, sin_t, gq, gkv, w_uq, w_uk):
    m = proj.shape[0]
    rb = _tile(m, 416, 16)
    half = D_ROPE // 2
    wq = w_uq.reshape(D_Q_RANK, D_HEADS, D_NOPE + D_ROPE)
    wn = wq[:, :, :D_NOPE].reshape(D_Q_RANK, D_HEADS * D_NOPE).astype(BF)
    rope = wq[:, :, D_NOPE:]
    pad = jnp.zeros((D_Q_RANK, D_HEADS, LANES - D_ROPE), F32)
    wr = jnp.concatenate([rope, pad], axis=-1).reshape(D_Q_RANK, D_HEADS * LANES).astype(BF)
    wrs = jnp.concatenate([rope[..., half:], rope[..., :half], pad], axis=-1).reshape(D_Q_RANK, D_HEADS * LANES).astype(BF)
    wuk = jnp.transpose(w_uk, (1, 2, 0)).astype(BF)
    blk = D_Q_RANK + D_KV_RANK
    row = lambda w: pl.BlockSpec((rb, w), lambda i: (i, 0))
    full2 = lambda a: pl.BlockSpec(a.shape, lambda i: (0, 0))
    return _pcall(
        _qkv_kernel, grid=(m // rb,),
        in_specs=[pl.BlockSpec((rb, blk), lambda i: (i, C_DQ // blk)), row(S_WIDTH), row(LANES), row(LANES),
                  pl.BlockSpec((1, D_Q_RANK), lambda i: (0, 0)), pl.BlockSpec((1, D_KV_RANK), lambda i: (0, 0)),
                  full2(wn), full2(wr), full2(wrs), pl.BlockSpec(wuk.shape, lambda i: (0, 0, 0))],
        out_specs=[row(D_KV_RANK), row(LANES), row(D_KV_RANK), row(LANES),
                   pl.BlockSpec((D_HEADS, rb, D_KV_RANK), lambda i: (0, i, 0)),
                   pl.BlockSpec((D_HEADS, rb, LANES), lambda i: (0, i, 0))],
        out_shape=[jax.ShapeDtypeStruct((m, D_KV_RANK), F32), jax.ShapeDtypeStruct((m, LANES), F32),
                   jax.ShapeDtypeStruct((m, D_KV_RANK), BF), jax.ShapeDtypeStruct((m, LANES), BF),
                   jax.ShapeDtypeStruct((D_HEADS, m, D_KV_RANK), BF),
                   jax.ShapeDtypeStruct((D_HEADS, m, LANES), BF)],
        name="mla_qkv_prep")(proj, small, cos_t, sin_t, gq.reshape(1, -1), gkv.reshape(1, -1), wn, wr, wrs, wuk)


def _attn_p_kernel(ql_ref, qr_ref, kv_ref, kr_ref, o_ref, m_scr, l_scr, acc_scr, *, tq):
    i = pl.program_id(1)
    scale = (D_NOPE + D_ROPE) ** -0.5
    m_scr[...] = jnp.full_like(m_scr, NEG_INF)
    l_scr[...] = jnp.zeros_like(l_scr)
    acc_scr[...] = jnp.zeros_like(acc_scr)
    q_pos = i * tq + lax.broadcasted_iota(jnp.int32, (tq, tq), 0)
    k_off = lax.broadcasted_iota(jnp.int32, (tq, tq), 1)

    def key_block(j, carry):
        rows = pl.ds(pl.multiple_of(j * tq, tq), tq)
        kv = kv_ref[rows, :]
        kr = kr_ref[rows, :]
        visible = j * tq + k_off <= q_pos
        for h in range(D_HEADS):
            s = (_dot_nt(ql_ref[h], kv) + _dot_nt(qr_ref[h], kr)) * scale
            s = jnp.where(visible, s, NEG_INF)
            m_prev = m_scr[h]
            m_new = jnp.maximum(m_prev, jnp.max(s, axis=-1, keepdims=True))
            alpha = jnp.exp(m_prev - m_new)
            e = jnp.exp(s - m_new)
            l_scr[h] = alpha * l_scr[h] + jnp.sum(e, axis=-1, keepdims=True)
            acc_scr[h] = alpha * acc_scr[h] + _dot(e.astype(BF), kv)
            m_scr[h] = m_new
        return carry

    lax.fori_loop(0, i + 1, key_block, 0)
    for h in range(D_HEADS):
        o_ref[h] = (acc_scr[h] / l_scr[h]).astype(BF)


def _attn_prompt(ql, qr, kvb, krb, bsz, t):
    tq = _tile(t, 256, 16)
    nq = t // tq
    return _pcall(
        functools.partial(_attn_p_kernel, tq=tq), grid=(bsz, nq),
        in_specs=[pl.BlockSpec((D_HEADS, tq, D_KV_RANK), lambda b, i: (0, b * nq + i, 0)),
                  pl.BlockSpec((D_HEADS, tq, LANES), lambda b, i: (0, b * nq + i, 0)),
                  pl.BlockSpec((t, D_KV_RANK), lambda b, i: (b, 0)),
                  pl.BlockSpec((t, LANES), lambda b, i: (b, 0))],
        out_specs=pl.BlockSpec((D_HEADS, tq, D_KV_RANK), lambda b, i: (0, b * nq + i, 0)),
        out_shape=jax.ShapeDtypeStruct((D_HEADS, bsz * t, D_KV_RANK), BF),
        scratch_shapes=[pltpu.VMEM((D_HEADS, tq, 1), F32), pltpu.VMEM((D_HEADS, tq, 1), F32),
                        pltpu.VMEM((D_HEADS, tq, D_KV_RANK), F32)],
        name="mla_attn_prompt")(ql, qr, kvb, krb)


def _decode_kernel(pt_ref, q_ref, qr_ref, *rest, n_pages):
    lat_refs = rest[:n_pages]
    kr_refs = rest[n_pages:2 * n_pages]
    ckv_ref, krn_ref, o_ref, m_scr, l_scr, acc_scr, kv_scr, kr_scr = rest[2 * n_pages:]
    g = pl.program_id(1)
    scale = (D_NOPE + D_ROPE) ** -0.5
    page = lat_refs[0].shape[0]

    @pl.when(g == 0)
    def _():
        m_scr[...] = jnp.full_like(m_scr, NEG_INF)
        l_scr[...] = jnp.zeros_like(l_scr)
        acc_scr[...] = jnp.zeros_like(acc_scr)

    for p in range(n_pages):
        kv_scr[p * page:(p + 1) * page, :] = lat_refs[p][...].astype(BF)
        kr_scr[:, p * page:(p + 1) * page] = kr_refs[p][...].astype(BF)
    q = q_ref[...]
    qr = qr_ref[:, :D_ROPE]
    kv = kv_scr[...]
    s = (_dot_nt(q, kv) + _dot(qr, kr_scr[...])) * scale
    m_prev = m_scr[...]
    m_new = jnp.maximum(m_prev, jnp.max(s, axis=-1, keepdims=True))
    alpha = jnp.exp(m_prev - m_new)
    e = jnp.exp(s - m_new)
    l_scr[...] = alpha * l_scr[...] + jnp.sum(e, axis=-1, keepdims=True)
    acc_scr[...] = alpha * acc_scr[...] + _dot(e.astype(BF), kv)
    m_scr[...] = m_new

    @pl.when(g == pl.num_programs(1) - 1)
    def _():
        ckv = ckv_ref[...]
        s_self = (jnp.sum(q.astype(F32) * ckv, axis=-1, keepdims=True)
                  + jnp.sum(qr.astype(F32) * krn_ref[:, :D_ROPE], axis=-1, keepdims=True)) * scale
        m_prev = m_scr[...]
        m_new = jnp.maximum(m_prev, s_self)
        alpha = jnp.exp(m_prev - m_new)
        e_self = jnp.exp(s_self - m_new)
        denom = alpha * l_scr[...] + e_self
        o_ref[...] = (alpha * acc_scr[...] + e_self * ckv) / denom


def _attn_decode(q8, qr8, cache_latent, cache_kr_t, layer, page_table, ckv_new, kr_new):
    n, n_tab = page_table.shape
    page = cache_latent.shape[2]
    pp = _tile(n_tab, PAGES_PER_STEP, 1)
    lat_specs = [pl.BlockSpec((None, None, page, D_KV_RANK),
                              functools.partial(lambda b, g, pt, p: (layer, pt[b, g * pp + p], 0, 0), p=p))
                 for p in range(pp)]
    kr_specs = [pl.BlockSpec((None, None, D_ROPE, page),
                             functools.partial(lambda b, g, pt, p: (layer, pt[b, g * pp + p], 0, 0), p=p))
                for p in range(pp)]
    grid_spec = pltpu.PrefetchScalarGridSpec(
        num_scalar_prefetch=1, grid=(n, n_tab // pp),
        in_specs=[pl.BlockSpec((None, D_HEADS, D_KV_RANK), lambda b, g, pt: (b, 0, 0)),
                  pl.BlockSpec((None, D_HEADS, LANES), lambda b, g, pt: (b, 0, 0))] + lat_specs + kr_specs
        + [pl.BlockSpec((None, 1, D_KV_RANK), lambda b, g, pt: (b, 0, 0)),
           pl.BlockSpec((None, 1, LANES), lambda b, g, pt: (b, 0, 0))],
        out_specs=pl.BlockSpec((None, D_HEADS, D_KV_RANK), lambda b, g, pt: (b, 0, 0)),
        scratch_shapes=[pltpu.VMEM((D_HEADS, 1), F32), pltpu.VMEM((D_HEADS, 1), F32),
                        pltpu.VMEM((D_HEADS, D_KV_RANK), F32),
                        pltpu.VMEM((pp * page, D_KV_RANK), BF), pltpu.VMEM((D_ROPE, pp * page), BF)])
    return _pcall(
        functools.partial(_decode_kernel, n_pages=pp), grid=(n, n_tab // pp), in_specs=None, out_specs=None,
        out_shape=jax.ShapeDtypeStruct((n, D_HEADS, D_KV_RANK), F32), grid_spec=grid_spec,
        name="mla_decode")(page_table, q8, qr8, *([cache_latent] * pp), *([cache_kr_t] * pp),
                           ckv_new.reshape(n, 1, D_KV_RANK), kr_new.reshape(n, 1, LANES))


def _uv_kernel(o_ref, w_ref, y_ref):
    for h in range(D_HEADS):
        y_ref[:, h * D_VDIM:(h + 1) * D_VDIM] = _dot(o_ref[h], w_ref[h]).astype(BF)


def _uv_proj(olat, w_uv):
    m = olat.shape[1]
    rb = _tile(m, 512, 16)
    wuv = jnp.transpose(w_uv, (1, 0, 2)).astype(BF)
    return _pcall(
        _uv_kernel, grid=(m // rb,),
        in_specs=[pl.BlockSpec((D_HEADS, rb, D_KV_RANK), lambda i: (0, i, 0)),
                  pl.BlockSpec(wuv.shape, lambda i: (0, 0, 0))],
        out_specs=pl.BlockSpec((rb, D_HEADS * D_VDIM), lambda i: (i, 0)),
        out_shape=jax.ShapeDtypeStruct((m, D_HEADS * D_VDIM), BF), name="mla_uv")(olat, wuv)


def kernel(x_prompt, x_sample, cache_latent, cache_k_rope, page_table, state_gla, state_conv, ffn1_norm_pre, ffn1_norm_post, ffn1_w_gu, ffn1_w_down, mix_norm_pre, mix_norm_post, w_in, a_ln_g, a_ln_b, a_w_s, a_b_s, b_w_alpha, b_b_alpha, b_norm_g, c_conv_w, c_conv_b, c_ln_g, c_ln_b, d_q_norm_g, d_kv_norm_g, d_w_uq, d_w_uk, d_w_uv, w_branch, w_out, ffn2_norm_pre, ffn2_norm_post, ffn2_w_gu, ffn2_w_down):
    bsz, t, d = x_prompt.shape
    n_s, t_s, _ = x_sample.shape
    depth = w_in.shape[0]
    assert t_s == 1 and n_s == A_CHUNK and t % A_CHUNK == 0
    mp = bsz * t
    past_len = page_table.shape[1] * cache_latent.shape[2]

    pos = jnp.concatenate([jnp.tile(jnp.arange(t, dtype=jnp.int32), bsz),
                           jnp.full((n_s,), past_len, jnp.int32)])
    cos_t, sin_t = _rope_tables(pos)
    w_in_t = jnp.swapaxes(w_in, 1, 2)
    cache_kr_t = jnp.swapaxes(cache_k_rope, 2, 3)

    h = jnp.concatenate([x_prompt.reshape(mp, d), x_sample.reshape(n_s, d)], axis=0)
    n = _norm_cast(h, ffn1_norm_pre[0])
    outs = [[] for _ in range(9)]
    for l in range(depth):
        hid = _mm_gu(n, ffn1_w_gu, l)
        f = _mm(hid, ffn1_w_down, l, F32, "ffn_down", single_buffer_x=True)
        h, n = _post(f, h, ffn1_norm_post[l], mix_norm_pre[l], 0.5)

        proj = _in_proj(n, w_in_t, l, d)
        small = _in_proj_small(n, w_in_t, l)

        y_a, v_s = _mixer_a(proj, mp // A_CHUNK, a_ln_g[l], a_ln_b[l], a_w_s[l], a_b_s[l])

        yb_p, gla_p = _gla_prompt(proj, small, bsz, t, b_w_alpha[l], b_b_alpha[l], b_norm_g[l])
        yb_s, gla_s = _gla_sample(proj, small, mp, n_s, b_w_alpha[l], b_b_alpha[l], b_norm_g[l], state_gla, l)
        y_b = jnp.concatenate([yb_p, yb_s], axis=0)

        yc_p, conv_p = _conv_prompt(proj, bsz, t, c_conv_w[l], c_conv_b[l], c_ln_g[l], c_ln_b[l])
        yc_s, conv_s = _conv_sample(proj, mp, n_s, state_conv, l, c_conv_w[l], c_conv_b[l], c_ln_g[l], c_ln_b[l])
        y_c = jnp.concatenate([yc_p, yc_s], axis=0)

        lat, kro, kvb, krb, ql, qr = _qkv_prep(proj, small, cos_t, sin_t, d_q_norm_g[l], d_kv_norm_g[l],
                                               d_w_uq[l], d_w_uk[l])
        o_p = _attn_prompt(ql, qr, kvb, krb, bsz, t)
        o_s = _attn_decode(jnp.transpose(ql[:, mp:], (1, 0, 2)), jnp.transpose(qr[:, mp:], (1, 0, 2)),
                           cache_latent, cache_kr_t, l, page_table, lat[mp:], kro[mp:])
        olat = jnp.concatenate([o_p, jnp.transpose(o_s, (1, 0, 2)).astype(BF)], axis=1)
        y_d = _uv_proj(olat, d_w_uv[l])

        merged = _merge((y_a, y_b, y_c, y_d), w_branch, l, proj, d)
        f = _mm(merged, w_out, l, F32, "out_proj")
        h, n = _post(f, h, mix_norm_post[l], ffn2_norm_pre[l], 1.0)

        hid = _mm_gu(n, ffn2_w_gu, l)
        f = _mm(hid, ffn2_w_down, l, F32, "ffn_down", single_buffer_x=True)
        h, n = _post(f, h, ffn2_norm_post[l], ffn1_norm_pre[(l + 1) % depth], 0.5)

        for acc, val in zip(outs, (lat[:mp].reshape(bsz, t, D_KV_RANK), lat[mp:].reshape(n_s, 1, D_KV_RANK),
                                   kro[:mp, :D_ROPE].reshape(bsz, t, D_ROPE), kro[mp:, :D_ROPE].reshape(n_s, 1, D_ROPE),
                                   gla_p, gla_s, conv_p, conv_s, v_s.reshape(n_s, 1, BRANCH_WIDTH))):
            acc.append(val)

    return (h[:mp].reshape(bsz, t, d), h[mp:].reshape(n_s, 1, d)) + tuple(jnp.stack(o) for o in outs)
```

```python
import functools

import jax
import jax.numpy as jnp
from jax import lax
from jax.experimental import pallas as pl
from jax.experimental.pallas import tpu as pltpu

F32 = jnp.float32
BF = jnp.bfloat16

EPS = 1e-6
N_BRANCH = 4
BRANCH_WIDTH = 1024
A_GROUPS = 4
A_CHUNK = 128
B_HEADS = 4
B_DK = 128
B_DV = 256
B_GATE_RANK = 16
B_GATE_TAU = 16.0
B_CHUNK = 64
CONV_WIDTH = 31
D_HEADS = 8
D_NOPE = 128
D_ROPE = 64
D_VDIM = 128
D_Q_RANK = 768
D_KV_RANK = 256
ROPE_BASE = 10000.0
NEG_INF = -1e30

LANES = 128
VMEM_LIMIT = 56 * 1024 * 1024
X_TILE_BYTES = 19 * 1024 * 1024
PAGES_PER_GROUP = 32
CONV_HALO = 32

C_AU, C_AV, C_BQ, C_BK, C_BV, C_BR, C_CA, C_CG, C_DQ, C_DKV, C_GATE = (
    0, 1024, 2048, 2560, 3072, 4096, 5120, 6144, 7168, 7936, 8192)
S_KR, S_KRS, S_BZ, S_WIDTH = 0, 64, 128, 256


def _tile(n, target, mult):
    best = None
    for t in range(mult, min(n, target) + 1, mult):
        if n % t == 0:
            best = t
    assert best is not None, (n, target, mult)
    return best


def _dot(a, b):
    return jnp.dot(a, b, preferred_element_type=F32)


def _dot_nt(a, b):
    return lax.dot_general(a, b, (((1,), (1,)), ((), ())), preferred_element_type=F32)


def _dot_tn(a, b):
    return lax.dot_general(a, b, (((0,), (0,)), ((), ())), preferred_element_type=F32)


def _split3(x):
    hi = x.astype(BF)
    r1 = x - hi.astype(F32)
    mid = r1.astype(BF)
    lo = (r1 - mid.astype(F32)).astype(BF)
    return hi, mid, lo


def _rms(x, g):
    return x * lax.rsqrt(jnp.mean(x * x, axis=-1, keepdims=True) + EPS) * g


def _layer_norm(x, g, b):
    mu = jnp.mean(x, axis=-1, keepdims=True)
    xc = x - mu
    var = jnp.mean(xc * xc, axis=-1, keepdims=True)
    return xc * lax.rsqrt(var + EPS) * g + b


def _silu(x):
    return x * jax.nn.sigmoid(x)


def _gelu(x):
    return jax.nn.gelu(x, approximate=True)


def _log_sigmoid(x):
    return jnp.minimum(x, 0.0) - jnp.log1p(jnp.exp(-jnp.abs(x)))


def _pcall(kernel, *, grid, in_specs, out_specs, out_shape, scratch_shapes=(), name=None, grid_spec=None):
    params = pltpu.CompilerParams(dimension_semantics=("arbitrary",) * len(grid), vmem_limit_bytes=VMEM_LIMIT)
    if grid_spec is not None:
        return pl.pallas_call(kernel, grid_spec=grid_spec, out_shape=out_shape, compiler_params=params, name=name)
    return pl.pallas_call(kernel, grid=grid, in_specs=in_specs, out_specs=out_specs, out_shape=out_shape,
                          scratch_shapes=scratch_shapes, compiler_params=params, name=name)


def _norm_kernel(x_ref, g_ref, o_ref):
    o_ref[...] = _rms(x_ref[...], g_ref[...]).astype(BF)


def _norm_cast(h, g):
    m, d = h.shape
    rb = _tile(m, 416, 16)
    return _pcall(
        _norm_kernel, grid=(m // rb,),
        in_specs=[pl.BlockSpec((rb, d), lambda i: (i, 0)), pl.BlockSpec((1, d), lambda i: (0, 0))],
        out_specs=pl.BlockSpec((rb, d), lambda i: (i, 0)),
        out_shape=jax.ShapeDtypeStruct((m, d), BF), name="norm_cast")(h, g.reshape(1, d))


def _post_kernel(f_ref, h_ref, gp_ref, gn_ref, ho_ref, no_ref, *, scale):
    h = h_ref[...] + scale * _rms(f_ref[...], gp_ref[...])
    ho_ref[...] = h
    no_ref[...] = _rms(h, gn_ref[...]).astype(BF)


def _post(f, h, g_post, g_next, scale):
    m, d = h.shape
    rb = _tile(m, 208, 16)
    row = pl.BlockSpec((rb, d), lambda i: (i, 0))
    vec = pl.BlockSpec((1, d), lambda i: (0, 0))
    return _pcall(
        functools.partial(_post_kernel, scale=scale), grid=(m // rb,),
        in_specs=[row, row, vec, vec], out_specs=[row, row],
        out_shape=[jax.ShapeDtypeStruct((m, d), F32), jax.ShapeDtypeStruct((m, d), BF)],
        name="post_norm")(f, h, g_post.reshape(1, d), g_next.reshape(1, d))


def _mm_kernel(x_ref, w_ref, o_ref):
    o_ref[...] = _dot(x_ref[...], w_ref[...].astype(BF)).astype(o_ref.dtype)


def _x_tile(m, k):
    tm = _tile(m, X_TILE_BYTES // (2 * k), 16)
    return tm, pl.BlockSpec((tm, k), lambda i, j: (i, 0), pipeline_mode=pl.Buffered(1))


def _mm(x, w, layer, out_dtype, name):
    m, k = x.shape
    n = w.shape[-1]
    tm, x_spec = _x_tile(m, k)
    tn = _tile(n, 512, 2 * LANES) if n % (2 * LANES) == 0 else _tile(n, 512, LANES)
    if k > 8192:
        tn = _tile(n, 256, LANES)
    w_spec = pl.BlockSpec((None, k, tn), lambda i, j: (layer, 0, j))
    return _pcall(
        _mm_kernel, grid=(m // tm, n // tn), in_specs=[x_spec, w_spec],
        out_specs=pl.BlockSpec((tm, tn), lambda i, j: (i, j)),
        out_shape=jax.ShapeDtypeStruct((m, n), out_dtype), name=name)(x, w)


def _mm_nt_kernel(x_ref, w_ref, o_ref):
    o_ref[...] = _dot_nt(x_ref[...], w_ref[...].astype(BF)).astype(o_ref.dtype)


W_IN_BZ = 4096
W_IN_BR = W_IN_BZ + B_GATE_RANK
W_IN_KR = W_IN_BR + 3 * BRANCH_WIDTH + D_Q_RANK + D_KV_RANK
W_IN_GATE = W_IN_KR + D_ROPE


def _in_proj(x, w_t, layer, d):
    m, k = x.shape
    tm, x_spec = _x_tile(m, k)
    tn = 512
    run0 = W_IN_BZ // tn
    run1 = run0 + (W_IN_KR - W_IN_BR) // tn
    n_out = W_IN_BZ + (W_IN_KR - W_IN_BR) + N_BRANCH * d
    assert W_IN_BZ % tn == 0 and (W_IN_KR - W_IN_BR) % tn == 0 and (N_BRANCH * d) % tn == 0

    sub = 8
    assert W_IN_BR % sub == 0 and W_IN_GATE % sub == 0

    def w_row(j):
        return sub * jnp.where(j < run0, j * (tn // sub),
                               jnp.where(j < run1, W_IN_BR // sub + (j - run0) * (tn // sub),
                                         W_IN_GATE // sub + (j - run1) * (tn // sub)))

    def nt_kernel(x_ref, w_ref, o_ref):
        _mm_nt_kernel(x_ref, w_ref.at[0], o_ref)

    return _pcall(
        nt_kernel, grid=(m // tm, n_out // tn),
        in_specs=[x_spec,
                  pl.BlockSpec((pl.Element(1), pl.Element(tn), pl.Element(k)),
                               lambda i, j: (layer, w_row(j), 0))],
        out_specs=pl.BlockSpec((tm, tn), lambda i, j: (i, j)),
        out_shape=jax.ShapeDtypeStruct((m, n_out), F32), name="in_proj")(x, w_t)


def _in_proj_small(x, w_t, layer):
    m, k = x.shape
    half = D_ROPE // 2
    w = w_t[layer]
    ws = jnp.concatenate(
        [w[W_IN_KR:W_IN_GATE], w[W_IN_KR + half:W_IN_GATE], w[W_IN_KR:W_IN_KR + half], w[W_IN_BZ:W_IN_BR],
         jnp.zeros((S_WIDTH - 2 * D_ROPE - B_GATE_RANK, k), F32)], axis=0)
    tm = _tile(m, 832, 16)
    return _pcall(
        _mm_nt_kernel, grid=(m // tm,),
        in_specs=[pl.BlockSpec((tm, k), lambda i: (i, 0)), pl.BlockSpec((S_WIDTH, k), lambda i: (0, 0))],
        out_specs=pl.BlockSpec((tm, S_WIDTH), lambda i: (i, 0)),
        out_shape=jax.ShapeDtypeStruct((m, S_WIDTH), F32), name="in_proj_small")(x, ws)


def _mm_gu_kernel(x_ref, wg_ref, wu_ref, o_ref):
    x = x_ref[...]
    a = _dot(x, wg_ref[...].astype(BF))
    b = _dot(x, wu_ref[...].astype(BF))
    o_ref[...] = (_silu(a) * b).astype(BF)


def _mm_gu(x, w_gu, layer):
    m, k = x.shape
    f = w_gu.shape[-1] // 2
    tm, x_spec = _x_tile(m, k)
    tn = _tile(f, 256, LANES)
    nj = f // tn
    return _pcall(
        _mm_gu_kernel, grid=(m // tm, nj),
        in_specs=[x_spec,
                  pl.BlockSpec((None, k, tn), lambda i, j: (layer, 0, j)),
                  pl.BlockSpec((None, k, tn), lambda i, j: (layer, 0, j + nj))],
        out_specs=pl.BlockSpec((tm, tn), lambda i, j: (i, j)),
        out_shape=jax.ShapeDtypeStruct((m, f), BF), name="ffn_gate_up")(x, w_gu, w_gu)


def _merge_kernel(ya_ref, yb_ref, yc_ref, yd_ref, w_ref, g0_ref, g1_ref, g2_ref, g3_ref, o_ref):
    acc = None
    for b, (y_ref, g_ref) in enumerate(zip((ya_ref, yb_ref, yc_ref, yd_ref), (g0_ref, g1_ref, g2_ref, g3_ref))):
        t = _dot(y_ref[...], w_ref[b].astype(BF)) * jax.nn.sigmoid(g_ref[...])
        acc = t if acc is None else acc + t
    o_ref[...] = acc.astype(BF)


def _merge(ys, w_branch, layer, proj, d):
    m = proj.shape[0]
    tm = _tile(m, 832, 16)
    tn = _tile(d, 512, LANES)
    y_spec = pl.BlockSpec((tm, BRANCH_WIDTH), lambda i, j: (i, 0))
    gate_specs = [pl.BlockSpec((tm, tn), functools.partial(lambda i, j, b: (i, (C_GATE + b * d) // tn + j), b=b))
                  for b in range(N_BRANCH)]
    return _pcall(
        _merge_kernel, grid=(m // tm, d // tn),
        in_specs=[y_spec] * N_BRANCH
        + [pl.BlockSpec((None, N_BRANCH, BRANCH_WIDTH, tn), lambda i, j: (layer, 0, 0, j))] + gate_specs,
        out_specs=pl.BlockSpec((tm, tn), lambda i, j: (i, j)),
        out_shape=jax.ShapeDtypeStruct((m, d), BF), name="gated_merge")(*ys, w_branch, proj, proj, proj, proj)


def _a_kernel(p_ref, lg_ref, lb_ref, ws_ref, bs_ref, y_ref, v_ref):
    u = _gelu(p_ref[:, :BRANCH_WIDTH])
    v = _layer_norm(_gelu(p_ref[:, BRANCH_WIDTH:]), lg_ref[...], lb_ref[...])
    v_ref[...] = v
    vb = v.astype(BF)
    gw = BRANCH_WIDTH // A_GROUPS
    for g in range(A_GROUPS):
        cols = slice(g * gw, (g + 1) * gw)
        mixed = _dot(ws_ref[g], vb[:, cols]) + bs_ref[g]
        y_ref[:, cols] = (u[:, cols] * mixed).astype(BF)


def _mixer_a(proj, n_prompt_chunks, ln_g, ln_b, w_s, b_s):
    m = proj.shape[0]
    c = A_CHUNK
    causal = jnp.tril(jnp.ones((c, c), dtype=bool))
    ws = jnp.stack([jnp.where(causal, w_s, 0.0),
                    w_s[:, :1, :1] * jnp.eye(c, dtype=F32)[None]]).astype(BF)
    bs = jnp.stack([b_s, jnp.broadcast_to(b_s[:, :1], b_s.shape)])[..., None]
    kind = lambda i: i // n_prompt_chunks
    y, v = _pcall(
        _a_kernel, grid=(m // c,),
        in_specs=[pl.BlockSpec((c, 2 * BRANCH_WIDTH), lambda i: (i, C_AU // (2 * BRANCH_WIDTH))),
                  pl.BlockSpec((1, BRANCH_WIDTH), lambda i: (0, 0)),
                  pl.BlockSpec((1, BRANCH_WIDTH), lambda i: (0, 0)),
                  pl.BlockSpec((None, A_GROUPS, c, c), lambda i: (kind(i), 0, 0, 0)),
                  pl.BlockSpec((None, A_GROUPS, c, 1), lambda i: (kind(i), 0, 0, 0))],
        out_specs=[pl.BlockSpec((c, BRANCH_WIDTH), lambda i: (i, 0)),
                   pl.BlockSpec((c, BRANCH_WIDTH), lambda i: (0, 0))],
        out_shape=[jax.ShapeDtypeStruct((m, BRANCH_WIDTH), BF), jax.ShapeDtypeStruct((c, BRANCH_WIDTH), F32)],
        name="mixer_a")(proj, ln_g.reshape(1, -1), ln_b.reshape(1, -1), ws, bs)
    return y, v


def _gla_log_decay(z_ref, wa_ref, ba_ref, rows):
    z = z_ref[rows, :].astype(BF)
    return _log_sigmoid(_dot(z, wa_ref[...]) + ba_ref[...]) * (1.0 / B_GATE_TAU)


def _gla_out(o, g, r):
    return _silu(r) * _rms(o, g)


def _gla_p_kernel(q_ref, k_ref, v_ref, r_ref, z_ref, wa_ref, ba_ref, g_ref, y_ref, so_ref, s_scr, *, n_chunks):
    blk = pl.program_id(1)
    c = B_CHUNK

    @pl.when(blk == 0)
    def _():
        s_scr[...] = jnp.zeros_like(s_scr)

    ri = lax.broadcasted_iota(jnp.int32, (c, c), 0)
    ci = lax.broadcasted_iota(jnp.int32, (c, c), 1)
    causal = ri >= ci
    tri = jnp.where(causal, 1.0, 0.0).astype(BF)
    ones = jnp.ones((c, B_DV), BF)

    def chunk(ic, carry):
        rows = pl.ds(pl.multiple_of(ic * c, c), c)
        la_all = _gla_log_decay(z_ref, wa_ref, ba_ref, rows)
        for h in range(B_HEADS):
            ks = slice(h * B_DK, (h + 1) * B_DK)
            vs = slice(h * B_DV, (h + 1) * B_DV)
            la = la_all[:, ks]
            hi, mid, lo = _split3(la)
            cum = _dot(tri, hi) + _dot(tri, mid) + _dot(tri, lo)
            tot_col = _dot_tn(hi, ones) + _dot_tn(mid, ones) + _dot_tn(lo, ones)
            q = q_ref[rows, ks] * (B_DK ** -0.5)
            k = k_ref[rows, ks]
            v = v_ref[rows, vs].astype(BF)
            q_t = (q * jnp.exp(cum)).astype(BF)
            k_t = (k * jnp.exp(-cum)).astype(BF)
            scores = jnp.where(causal, _dot_nt(q_t, k_t), 0.0)
            s = s_scr[h]
            o = _dot(scores.astype(BF), v) + _dot(q_t, s.astype(BF))
            total = cum[c - 1:c, :]
            k_end = (k * jnp.exp(total - cum)).astype(BF)
            s_scr[h] = jnp.exp(tot_col) * s + _dot_tn(k_end, v)
            y_ref[rows, vs] = _gla_out(o, g_ref[:, vs], r_ref[rows, vs]).astype(BF)
        return carry

    lax.fori_loop(0, n_chunks, chunk, 0)
    so_ref[...] = s_scr[...]


def _gla_alpha(w_alpha, b_alpha):
    wa = jnp.zeros((S_WIDTH, B_HEADS * B_DK), F32).at[S_BZ:S_BZ + B_GATE_RANK].set(w_alpha).astype(BF)
    return wa, b_alpha.reshape(1, -1)


def _gla_prompt(proj, small, bsz, t, w_alpha, b_alpha, norm_g):
    tb = _tile(t, 512, B_CHUNK)
    nb = t // tb
    hk = B_HEADS * B_DK
    hv = B_HEADS * B_DV
    wa, ba = _gla_alpha(w_alpha, b_alpha)
    row = lambda b, i: b * nb + i
    y, s = _pcall(
        functools.partial(_gla_p_kernel, n_chunks=tb // B_CHUNK), grid=(bsz, nb),
        in_specs=[pl.BlockSpec((tb, hk), lambda b, i: (row(b, i), C_BQ // hk)),
                  pl.BlockSpec((tb, hk), lambda b, i: (row(b, i), C_BK // hk)),
                  pl.BlockSpec((tb, hv), lambda b, i: (row(b, i), C_BV // hv)),
                  pl.BlockSpec((tb, hv), lambda b, i: (row(b, i), C_BR // hv)),
                  pl.BlockSpec((tb, S_WIDTH), lambda b, i: (row(b, i), 0)),
                  pl.BlockSpec((S_WIDTH, hk), lambda b, i: (0, 0)),
                  pl.BlockSpec((1, hk), lambda b, i: (0, 0)),
                  pl.BlockSpec((1, hv), lambda b, i: (0, 0))],
        out_specs=[pl.BlockSpec((tb, hv), lambda b, i: (row(b, i), 0)),
                   pl.BlockSpec((None, B_HEADS, B_DK, B_DV), lambda b, i: (b, 0, 0, 0))],
        out_shape=[jax.ShapeDtypeStruct((bsz * t, hv), BF),
                   jax.ShapeDtypeStruct((bsz, B_HEADS, B_DK, B_DV), F32)],
        scratch_shapes=[pltpu.VMEM((B_HEADS, B_DK, B_DV), F32)],
        name="gla_prompt")(proj, proj, proj, proj, small, wa, ba, norm_g.reshape(1, -1))
    return y, s


def _gla_s_kernel(q_ref, k_ref, v_ref, r_ref, z_ref, wa_ref, ba_ref, g_ref, s_ref, y_ref, so_ref, *, bb):
    la_all = _gla_log_decay(z_ref, wa_ref, ba_ref, slice(None))
    ri = lax.broadcasted_iota(jnp.int32, (B_DK, B_DK), 0)
    ci = lax.broadcasted_iota(jnp.int32, (B_DK, B_DK), 1)
    eye = ri == ci
    ones = jnp.ones((B_DK, B_DV), BF)

    def diag(row):
        return jnp.where(eye, jnp.broadcast_to(row, (B_DK, B_DK)), 0.0)

    for b in range(bb):
        rs = slice(b, b + 1)
        for h in range(B_HEADS):
            ks = slice(h * B_DK, (h + 1) * B_DK)
            vs = slice(h * B_DV, (h + 1) * B_DV)
            la = la_all[rs, ks]
            k = k_ref[rs, ks]
            v = v_ref[rs, vs]
            q_t = q_ref[rs, ks] * (B_DK ** -0.5) * jnp.exp(la)
            k_t = k * jnp.exp(-la)
            score = jnp.sum(q_t * k_t, axis=-1, keepdims=True)
            s0 = s_ref[b, h]
            q8 = jnp.broadcast_to(q_t, (8, B_DK)).astype(BF)
            o = score * v + _dot(q8, s0.astype(BF))[0:1]
            hi, mid, lo = _split3(diag(la))
            la_col = _dot(hi, ones) + _dot(mid, ones) + _dot(lo, ones)
            kv = _dot(diag(k).astype(BF), jnp.broadcast_to(v, (B_DK, B_DV)).astype(BF))
            so_ref[b, h] = jnp.exp(la_col) * s0 + kv
            y_ref[rs, vs] = _gla_out(o, g_ref[:, vs], r_ref[rs, vs])


def _gla_sample(proj, small, row0, n, w_alpha, b_alpha, norm_g, state, layer):
    bb = 8
    hk = B_HEADS * B_DK
    hv = B_HEADS * B_DV
    wa, ba = _gla_alpha(w_alpha, b_alpha)
    r0 = row0 // bb
    y, s = _pcall(
        functools.partial(_gla_s_kernel, bb=bb), grid=(n // bb,),
        in_specs=[pl.BlockSpec((bb, hk), lambda i: (r0 + i, C_BQ // hk)),
                  pl.BlockSpec((bb, hk), lambda i: (r0 + i, C_BK // hk)),
                  pl.BlockSpec((bb, hv), lambda i: (r0 + i, C_BV // hv)),
                  pl.BlockSpec((bb, hv), lambda i: (r0 + i, C_BR // hv)),
                  pl.BlockSpec((bb, S_WIDTH), lambda i: (r0 + i, 0)),
                  pl.BlockSpec((S_WIDTH, hk), lambda i: (0, 0)),
                  pl.BlockSpec((1, hk), lambda i: (0, 0)),
                  pl.BlockSpec((1, hv), lambda i: (0, 0)),
                  pl.BlockSpec((None, bb, B_HEADS, B_DK, B_DV), lambda i: (layer, i, 0, 0, 0))],
        out_specs=[pl.BlockSpec((bb, hv), lambda i: (i, 0)),
                   pl.BlockSpec((bb, B_HEADS, B_DK, B_DV), lambda i: (i, 0, 0, 0))],
        out_shape=[jax.ShapeDtypeStruct((n, hv), F32),
                   jax.ShapeDtypeStruct((n, B_HEADS, B_DK, B_DV), F32)],
        name="gla_sample")(proj, proj, proj, proj, small, wa, ba, norm_g.reshape(1, -1), state)
    return y.astype(BF), s


def _conv_p_kernel(a_ref, g_ref, ah_ref, gh_ref, w_ref, cb_ref, lg_ref, lb_ref, y_ref, st_ref, buf, *, tb):
    i = pl.program_id(1)
    glu = a_ref[...] * jax.nn.sigmoid(g_ref[...])
    halo = ah_ref[...] * jax.nn.sigmoid(gh_ref[...])
    buf[0:CONV_HALO, :] = jnp.where(i > 0, halo, 0.0)
    buf[CONV_HALO:, :] = glu
    acc = jnp.broadcast_to(cb_ref[...], glu.shape)
    off = CONV_HALO - (CONV_WIDTH - 1)
    for j in range(CONV_WIDTH):
        acc = acc + w_ref[j:j + 1, :] * buf[off + j:off + j + tb, :]
    y_ref[...] = _silu(_layer_norm(acc, lg_ref[...], lb_ref[...])).astype(BF)
    st_ref[...] = glu[tb - CONV_HALO:, :]


def _conv_prompt(proj, bsz, t, conv_w, conv_b, ln_g, ln_b):
    w = BRANCH_WIDTH
    tb = _tile(t, 256, CONV_HALO)
    nb = t // tb
    per = tb // CONV_HALO
    row = lambda b, i: b * nb + i
    halo = lambda b, i: jnp.maximum((b * nb + i) * per - 1, 0)
    vec = pl.BlockSpec((1, w), lambda b, i: (0, 0))
    y, st = _pcall(
        functools.partial(_conv_p_kernel, tb=tb), grid=(bsz, nb),
        in_specs=[pl.BlockSpec((tb, w), lambda b, i: (row(b, i), C_CA // w)),
                  pl.BlockSpec((tb, w), lambda b, i: (row(b, i), C_CG // w)),
                  pl.BlockSpec((CONV_HALO, w), lambda b, i: (halo(b, i), C_CA // w)),
                  pl.BlockSpec((CONV_HALO, w), lambda b, i: (halo(b, i), C_CG // w)),
                  pl.BlockSpec((CONV_WIDTH, w), lambda b, i: (0, 0)), vec, vec, vec],
        out_specs=[pl.BlockSpec((tb, w), lambda b, i: (row(b, i), 0)),
                   pl.BlockSpec((None, CONV_HALO, w), lambda b, i: (b, 0, 0))],
        out_shape=[jax.ShapeDtypeStruct((bsz * t, w), BF), jax.ShapeDtypeStruct((bsz, CONV_HALO, w), F32)],
        scratch_shapes=[pltpu.VMEM((tb + CONV_HALO, w), F32)],
        name="conv_prompt")(proj, proj, proj, proj, conv_w, conv_b.reshape(1, w), ln_g.reshape(1, w),
                            ln_b.reshape(1, w))
    return y, st[:, CONV_HALO - (CONV_WIDTH - 1):, :]


def _conv_s_kernel(a_ref, g_ref, st_ref, w_ref, cb_ref, lg_ref, lb_ref, y_ref, so_ref):
    nbuf = CONV_WIDTH - 1
    glu = a_ref[...] * jax.nn.sigmoid(g_ref[...])
    acc = cb_ref[...] + w_ref[nbuf:nbuf + 1, :] * glu
    for j in range(nbuf):
        acc = acc + w_ref[j:j + 1, :] * st_ref[j]
    for j in range(nbuf - 1):
        so_ref[j] = st_ref[j + 1]
    so_ref[nbuf - 1] = glu
    y_ref[...] = _silu(_layer_norm(acc, lg_ref[...], lb_ref[...]))


def _conv_sample(proj, row0, n, state, layer, conv_w, conv_b, ln_g, ln_b):
    w = BRANCH_WIDTH
    nbuf = CONV_WIDTH - 1
    bb = _tile(n, 32, 8)
    r0 = row0 // bb
    st = jnp.transpose(state, (0, 2, 1, 3))
    vec = pl.BlockSpec((1, w), lambda i: (0, 0))
    y, so = _pcall(
        _conv_s_kernel, grid=(n // bb,),
        in_specs=[pl.BlockSpec((bb, w), lambda i: (r0 + i, C_CA // w)),
                  pl.BlockSpec((bb, w), lambda i: (r0 + i, C_CG // w)),
                  pl.BlockSpec((None, nbuf, bb, w), lambda i: (layer, 0, i, 0)),
                  pl.BlockSpec((CONV_WIDTH, w), lambda i: (0, 0)), vec, vec, vec],
        out_specs=[pl.BlockSpec((bb, w), lambda i: (i, 0)),
                   pl.BlockSpec((nbuf, bb, w), lambda i: (0, i, 0))],
        out_shape=[jax.ShapeDtypeStruct((n, w), F32), jax.ShapeDtypeStruct((nbuf, n, w), F32)],
        name="conv_sample")(proj, proj, st, conv_w, conv_b.reshape(1, w), ln_g.reshape(1, w), ln_b.reshape(1, w))
    return y.astype(BF), jnp.transpose(so, (1, 0, 2))


def _qkv_kernel(pq_ref, sm_ref, ct_ref, st_ref, gq_ref, gkv_ref, wn_ref, wr_ref, wrs_ref, wuk_ref,
                lat_ref, kro_ref, kvb_ref, krb_ref, ql_ref, qr_ref):
    cq = _rms(pq_ref[:, :D_Q_RANK], gq_ref[...]).astype(BF)
    ckv = _rms(pq_ref[:, D_Q_RANK:], gkv_ref[...])
    lat_ref[...] = ckv
    kvb_ref[...] = ckv.astype(BF)
    cos = ct_ref[...]
    sin = st_ref[...]
    s = sm_ref[:, :LANES]
    kr = s * cos + pltpu.roll(s, D_ROPE, 1) * sin
    kro_ref[...] = kr
    krb_ref[...] = kr.astype(BF)
    qn = _dot(cq, wn_ref[...])
    q1 = _dot(cq, wr_ref[...])
    q2 = _dot(cq, wrs_ref[...])
    for h in range(D_HEADS):
        hs = slice(h * LANES, (h + 1) * LANES)
        ql_ref[h] = _dot(qn[:, hs].astype(BF), wuk_ref[h]).astype(BF)
        qr_ref[h] = (q1[:, hs] * cos + q2[:, hs] * sin).astype(BF)


def _rope_tables(pos):
    half = D_ROPE // 2
    inv_freq = ROPE_BASE ** (-jnp.arange(half, dtype=F32) / half)
    ang = pos.astype(F32)[:, None] * inv_freq[None, :]
    cos = jnp.cos(ang)
    sin = jnp.sin(ang)
    zero = jnp.zeros((pos.shape[0], LANES - D_ROPE), F32)
    return (jnp.concatenate([cos, cos, zero], axis=1), jnp.concatenate([-sin, sin, zero], axis=1))


def _qkv_prep(proj, small, cos_t, sin_t, gq, gkv, w_uq, w_uk):
    m = proj.shape[0]
    rb = _tile(m, 416, 16)
    half = D_ROPE // 2
    wq = w_uq.reshape(D_Q_RANK, D_HEADS, D_NOPE + D_ROPE)
    wn = wq[:, :, :D_NOPE].reshape(D_Q_RANK, D_HEADS * D_NOPE).astype(BF)
    rope = wq[:, :, D_NOPE:]
    pad = jnp.zeros((D_Q_RANK, D_HEADS, LANES - D_ROPE), F32)
    wr = jnp.concatenate([rope, pad], axis=-1).reshape(D_Q_RANK, D_HEADS * LANES).astype(BF)
    wrs = jnp.concatenate([rope[..., half:], rope[..., :half], pad], axis=-1).reshape(D_Q_RANK, D_HEADS * LANES).astype(BF)
    wuk = jnp.transpose(w_uk, (1, 2, 0)).astype(BF)
    blk = D_Q_RANK + D_KV_RANK
    row = lambda w: pl.BlockSpec((rb, w), lambda i: (i, 0))
    full2 = lambda a: pl.BlockSpec(a.shape, lambda i: (0, 0))
    return _pcall(
        _qkv_kernel, grid=(m // rb,),
        in_specs=[pl.BlockSpec((rb, blk), lambda i: (i, C_DQ // blk)), row(S_WIDTH), row(LANES), row(LANES),
                  pl.BlockSpec((1, D_Q_RANK), lambda i: (0, 0)), pl.BlockSpec((1, D_KV_RANK), lambda i: (0, 0)),
                  full2(wn), full2(wr), full2(wrs), pl.BlockSpec(wuk.shape, lambda i: (0, 0, 0))],
        out_specs=[row(D_KV_RANK), row(LANES), row(D_KV_RANK), row(LANES),
                   pl.BlockSpec((D_HEADS, rb, D_KV_RANK), lambda i: (0, i, 0)),
                   pl.BlockSpec((D_HEADS, rb, LANES), lambda i: (0, i, 0))],
        out_shape=[jax.ShapeDtypeStruct((m, D_KV_RANK), F32), jax.ShapeDtypeStruct((m, LANES), F32),
                   jax.ShapeDtypeStruct((m, D_KV_RANK), BF), jax.ShapeDtypeStruct((m, LANES), BF),
                   jax.ShapeDtypeStruct((D_HEADS, m, D_KV_RANK), BF),
                   jax.ShapeDtypeStruct((D_HEADS, m, LANES), BF)],
        name="mla_qkv_prep")(proj, small, cos_t, sin_t, gq.reshape(1, -1), gkv.reshape(1, -1), wn, wr, wrs, wuk)


def _attn_p_kernel(ql_ref, qr_ref, kv_ref, kr_ref, o_ref, m_scr, l_scr, acc_scr, *, tq):
    i = pl.program_id(1)
    scale = (D_NOPE + D_ROPE) ** -0.5
    m_scr[...] = jnp.full_like(m_scr, NEG_INF)
    l_scr[...] = jnp.zeros_like(l_scr)
    acc_scr[...] = jnp.zeros_like(acc_scr)
    q_pos = i * tq + lax.broadcasted_iota(jnp.int32, (tq, tq), 0)
    k_off = lax.broadcasted_iota(jnp.int32, (tq, tq), 1)

    def key_block(j, carry):
        rows = pl.ds(pl.multiple_of(j * tq, tq), tq)
        kv = kv_ref[rows, :]
        kr = kr_ref[rows, :]
        visible = j * tq + k_off <= q_pos
        for h in range(D_HEADS):
            s = (_dot_nt(ql_ref[h], kv) + _dot_nt(qr_ref[h], kr)) * scale
            s = jnp.where(visible, s, NEG_INF)
            m_prev = m_scr[h]
            m_new = jnp.maximum(m_prev, jnp.max(s, axis=-1, keepdims=True))
            alpha = jnp.exp(m_prev - m_new)
            e = jnp.exp(s - m_new)
            l_scr[h] = alpha * l_scr[h] + jnp.sum(e, axis=-1, keepdims=True)
            acc_scr[h] = alpha * acc_scr[h] + _dot(e.astype(BF), kv)
            m_scr[h] = m_new
        return carry

    lax.fori_loop(0, i + 1, key_block, 0)
    for h in range(D_HEADS):
        o_ref[h] = (acc_scr[h] / l_scr[h]).astype(BF)


def _attn_prompt(ql, qr, kvb, krb, bsz, t):
    tq = _tile(t, 512, 16)
    nq = t // tq
    return _pcall(
        functools.partial(_attn_p_kernel, tq=tq), grid=(bsz, nq),
        in_specs=[pl.BlockSpec((D_HEADS, tq, D_KV_RANK), lambda b, i: (0, b * nq + i, 0)),
                  pl.BlockSpec((D_HEADS, tq, LANES), lambda b, i: (0, b * nq + i, 0)),
                  pl.BlockSpec((t, D_KV_RANK), lambda b, i: (b, 0)),
                  pl.BlockSpec((t, LANES), lambda b, i: (b, 0))],
        out_specs=pl.BlockSpec((D_HEADS, tq, D_KV_RANK), lambda b, i: (0, b * nq + i, 0)),
        out_shape=jax.ShapeDtypeStruct((D_HEADS, bsz * t, D_KV_RANK), BF),
        scratch_shapes=[pltpu.VMEM((D_HEADS, tq, 1), F32), pltpu.VMEM((D_HEADS, tq, 1), F32),
                        pltpu.VMEM((D_HEADS, tq, D_KV_RANK), F32)],
        name="mla_attn_prompt")(ql, qr, kvb, krb)


def _decode_kernel(pt_ref, q_ref, qr_ref, ckv_ref, krn_ref, lat_hbm, kr_hbm, o_ref,
                   lat_buf, kr_buf, kv_scr, kr_scr, sem, *, layer, pages, n_groups):
    b = pl.program_id(0)
    scale = (D_NOPE + D_ROPE) ** -0.5
    page = lat_buf.shape[2]

    def group_copies(seq, g, slot):
        copies = []
        for p in range(pages):
            idx = pt_ref[seq, g * pages + p]
            copies.append(pltpu.make_async_copy(lat_hbm.at[layer, idx], lat_buf.at[slot, p], sem.at[0, slot]))
            copies.append(pltpu.make_async_copy(kr_hbm.at[layer, idx], kr_buf.at[slot, p], sem.at[1, slot]))
        return copies

    @pl.when(b == 0)
    def _():
        for c in group_copies(0, 0, 0):
            c.start()

    q = q_ref[...]
    qr = qr_ref[:, :D_ROPE]
    m = jnp.full((D_HEADS, 1), NEG_INF, F32)
    l = jnp.zeros((D_HEADS, 1), F32)
    acc = jnp.zeros((D_HEADS, D_KV_RANK), F32)
    for g in range(n_groups):
        slot = g % 2
        if g + 1 < n_groups:
            for c in group_copies(b, g + 1, 1 - slot):
                c.start()
        else:
            @pl.when(b + 1 < pl.num_programs(0))
            def _():
                for c in group_copies(b + 1, 0, 1 - slot):
                    c.start()
        for c in group_copies(b, g, slot):
            c.wait()
        for p in range(pages):
            kv_scr[p * page:(p + 1) * page, :] = lat_buf[slot, p].astype(BF)
            kr_scr[:, p * page:(p + 1) * page] = kr_buf[slot, p].astype(BF)
        kv = kv_scr[...]
        s = (_dot_nt(q, kv) + _dot(qr, kr_scr[...])) * scale
        m_new = jnp.maximum(m, jnp.max(s, axis=-1, keepdims=True))
        alpha = jnp.exp(m - m_new)
        e = jnp.exp(s - m_new)
        l = alpha * l + jnp.sum(e, axis=-1, keepdims=True)
        acc = alpha * acc + _dot(e.astype(BF), kv)
        m = m_new

    ckv = ckv_ref[...]
    s_self = (jnp.sum(q.astype(F32) * ckv, axis=-1, keepdims=True)
              + jnp.sum(qr.astype(F32) * krn_ref[:, :D_ROPE], axis=-1, keepdims=True)) * scale
    m_new = jnp.maximum(m, s_self)
    alpha = jnp.exp(m - m_new)
    e_self = jnp.exp(s_self - m_new)
    o_ref[...] = (alpha * acc + e_self * ckv) / (alpha * l + e_self)


def _attn_decode(q8, qr8, cache_latent, cache_kr_t, layer, page_table, ckv_new, kr_new):
    n, n_tab = page_table.shape
    page = cache_latent.shape[2]
    pages = _tile(n_tab // 2, PAGES_PER_GROUP, 1)
    n_groups = n_tab // pages
    assert n_groups % 2 == 0
    seq = lambda last: pl.BlockSpec((None,) + last, lambda b, pt: (b, 0, 0))
    grid_spec = pltpu.PrefetchScalarGridSpec(
        num_scalar_prefetch=1, grid=(n,),
        in_specs=[seq((D_HEADS, D_KV_RANK)), seq((D_HEADS, LANES)), seq((1, D_KV_RANK)), seq((1, LANES)),
                  pl.BlockSpec(memory_space=pl.ANY), pl.BlockSpec(memory_space=pl.ANY)],
        out_specs=seq((D_HEADS, D_KV_RANK)),
        scratch_shapes=[pltpu.VMEM((2, pages, page, D_KV_RANK), F32), pltpu.VMEM((2, pages, D_ROPE, page), F32),
                        pltpu.VMEM((pages * page, D_KV_RANK), BF), pltpu.VMEM((D_ROPE, pages * page), BF),
                        pltpu.SemaphoreType.DMA((2, 2))])
    return _pcall(
        functools.partial(_decode_kernel, layer=layer, pages=pages, n_groups=n_groups), grid=(n,),
        in_specs=None, out_specs=None, out_shape=jax.ShapeDtypeStruct((n, D_HEADS, D_KV_RANK), F32),
        grid_spec=grid_spec, name="mla_decode")(
            page_table, q8, qr8, ckv_new.reshape(n, 1, D_KV_RANK), kr_new.reshape(n, 1, LANES),
            cache_latent, cache_kr_t)


def _uv_kernel(o_ref, w_ref, y_ref):
    for h in range(D_HEADS):
        y_ref[:, h * D_VDIM:(h + 1) * D_VDIM] = _dot(o_ref[h], w_ref[h]).astype(BF)


def _uv_proj(olat, w_uv):
    m = olat.shape[1]
    rb = _tile(m, 512, 16)
    wuv = jnp.transpose(w_uv, (1, 0, 2)).astype(BF)
    return _pcall(
        _uv_kernel, grid=(m // rb,),
        in_specs=[pl.BlockSpec((D_HEADS, rb, D_KV_RANK), lambda i: (0, i, 0)),
                  pl.BlockSpec(wuv.shape, lambda i: (0, 0, 0))],
        out_specs=pl.BlockSpec((rb, D_HEADS * D_VDIM), lambda i: (i, 0)),
        out_shape=jax.ShapeDtypeStruct((m, D_HEADS * D_VDIM), BF), name="mla_uv")(olat, wuv)


def kernel(x_prompt, x_sample, cache_latent, cache_k_rope, page_table, state_gla, state_conv, ffn1_norm_pre, ffn1_norm_post, ffn1_w_gu, ffn1_w_down, mix_norm_pre, mix_norm_post, w_in, a_ln_g, a_ln_b, a_w_s, a_b_s, b_w_alpha, b_b_alpha, b_norm_g, c_conv_w, c_conv_b, c_ln_g, c_ln_b, d_q_norm_g, d_kv_norm_g, d_w_uq, d_w_uk, d_w_uv, w_branch, w_out, ffn2_norm_pre, ffn2_norm_post, ffn2_w_gu, ffn2_w_down):
    bsz, t, d = x_prompt.shape
    n_s, t_s, _ = x_sample.shape
    depth = w_in.shape[0]
    assert t_s == 1 and n_s == A_CHUNK and t % A_CHUNK == 0
    mp = bsz * t
    past_len = page_table.shape[1] * cache_latent.shape[2]

    pos = jnp.concatenate([jnp.tile(jnp.arange(t, dtype=jnp.int32), bsz),
                           jnp.full((n_s,), past_len, jnp.int32)])
    cos_t, sin_t = _rope_tables(pos)
    w_in_t = jnp.swapaxes(w_in, 1, 2)
    cache_kr_t = jnp.swapaxes(cache_k_rope, 2, 3)

    h = jnp.concatenate([x_prompt.reshape(mp, d), x_sample.reshape(n_s, d)], axis=0)
    n = _norm_cast(h, ffn1_norm_pre[0])
    outs = [[] for _ in range(9)]
    for l in range(depth):
        hid = _mm_gu(n, ffn1_w_gu, l)
        f = _mm(hid, ffn1_w_down, l, F32, "ffn_down")
        h, n = _post(f, h, ffn1_norm_post[l], mix_norm_pre[l], 0.5)

        proj = _in_proj(n, w_in_t, l, d)
        small = _in_proj_small(n, w_in_t, l)

        y_a, v_s = _mixer_a(proj, mp // A_CHUNK, a_ln_g[l], a_ln_b[l], a_w_s[l], a_b_s[l])

        yb_p, gla_p = _gla_prompt(proj, small, bsz, t, b_w_alpha[l], b_b_alpha[l], b_norm_g[l])
        yb_s, gla_s = _gla_sample(proj, small, mp, n_s, b_w_alpha[l], b_b_alpha[l], b_norm_g[l], state_gla, l)
        y_b = jnp.concatenate([yb_p, yb_s], axis=0)

        yc_p, conv_p = _conv_prompt(proj, bsz, t, c_conv_w[l], c_conv_b[l], c_ln_g[l], c_ln_b[l])
        yc_s, conv_s = _conv_sample(proj, mp, n_s, state_conv, l, c_conv_w[l], c_conv_b[l], c_ln_g[l], c_ln_b[l])
        y_c = jnp.concatenate([yc_p, yc_s], axis=0)

        lat, kro, kvb, krb, ql, qr = _qkv_prep(proj, small, cos_t, sin_t, d_q_norm_g[l], d_kv_norm_g[l],
                                               d_w_uq[l], d_w_uk[l])
        o_p = _attn_prompt(ql, qr, kvb, krb, bsz, t)
        o_s = _attn_decode(jnp.transpose(ql[:, mp:], (1, 0, 2)), jnp.transpose(qr[:, mp:], (1, 0, 2)),
                           cache_latent, cache_kr_t, l, page_table, lat[mp:], kro[mp:])
        olat = jnp.concatenate([o_p, jnp.transpose(o_s, (1, 0, 2)).astype(BF)], axis=1)
        y_d = _uv_proj(olat, d_w_uv[l])

        merged = _merge((y_a, y_b, y_c, y_d), w_branch, l, proj, d)
        f = _mm(merged, w_out, l, F32, "out_proj")
        h, n = _post(f, h, mix_norm_post[l], ffn2_norm_pre[l], 1.0)

        hid = _mm_gu(n, ffn2_w_gu, l)
        f = _mm(hid, ffn2_w_down, l, F32, "ffn_down")
        h, n = _post(f, h, ffn2_norm_post[l], ffn1_norm_pre[(l + 1) % depth], 0.5)

        for acc, val in zip(outs, (lat[:mp].reshape(bsz, t, D_KV_RANK), lat[mp:].reshape(n_s, 1, D_KV_RANK),
                                   kro[:mp, :D_ROPE].reshape(bsz, t, D_ROPE), kro[mp:, :D_ROPE].reshape(n_s, 1, D_ROPE),
                                   gla_p, gla_s, conv_p, conv_s, v_s.reshape(n_s, 1, BRANCH_WIDTH))):
            acc.append(val)

    return (h[:mp].reshape(bsz, t, d), h[mp:].reshape(n_s, 1, d)) + tuple(jnp.stack(o) for o in outs)
```

```python
import functools

import jax
import jax.numpy as jnp
from jax import lax
from jax.experimental import pallas as pl
from jax.experimental.pallas import tpu as pltpu

F32 = jnp.float32
BF = jnp.bfloat16

EPS = 1e-6
N_BRANCH = 4
BRANCH_WIDTH = 1024
A_GROUPS = 4
A_CHUNK = 128
B_HEADS = 4
B_DK = 128
B_DV = 256
B_GATE_RANK = 16
B_GATE_TAU = 16.0
B_CHUNK = 64
CONV_WIDTH = 31
D_HEADS = 8
D_NOPE = 128
D_ROPE = 64
D_VDIM = 128
D_Q_RANK = 768
D_KV_RANK = 256
ROPE_BASE = 10000.0
NEG_INF = -1e30

LANES = 128
VMEM_LIMIT = 56 * 1024 * 1024
X_TILE_BYTES = 19 * 1024 * 1024
PAGES_PER_GROUP = 32
CONV_HALO = 32

C_AU, C_AV, C_BQ, C_BK, C_BV, C_BR, C_CA, C_CG, C_DQ, C_DKV, C_GATE = (
    0, 1024, 2048, 2560, 3072, 4096, 5120, 6144, 7168, 7936, 8192)
S_KR, S_KRS, S_BZ, S_WIDTH = 0, 64, 128, 256


def _tile(n, target, mult):
    best = None
    for t in range(mult, min(n, target) + 1, mult):
        if n % t == 0:
            best = t
    assert best is not None, (n, target, mult)
    return best


def _dot(a, b):
    return jnp.dot(a, b, preferred_element_type=F32)


def _dot_nt(a, b):
    return lax.dot_general(a, b, (((1,), (1,)), ((), ())), preferred_element_type=F32)


def _dot_tn(a, b):
    return lax.dot_general(a, b, (((0,), (0,)), ((), ())), preferred_element_type=F32)


def _split3(x):
    hi = x.astype(BF)
    r1 = x - hi.astype(F32)
    mid = r1.astype(BF)
    lo = (r1 - mid.astype(F32)).astype(BF)
    return hi, mid, lo


def _rms(x, g):
    return x * lax.rsqrt(jnp.mean(x * x, axis=-1, keepdims=True) + EPS) * g


def _layer_norm(x, g, b):
    mu = jnp.mean(x, axis=-1, keepdims=True)
    xc = x - mu
    var = jnp.mean(xc * xc, axis=-1, keepdims=True)
    return xc * lax.rsqrt(var + EPS) * g + b


def _silu(x):
    return x * jax.nn.sigmoid(x)


def _gelu(x):
    return jax.nn.gelu(x, approximate=True)


def _log_sigmoid(x):
    return jnp.minimum(x, 0.0) - jnp.log1p(jnp.exp(-jnp.abs(x)))


def _pcall(kernel, *, grid, in_specs, out_specs, out_shape, scratch_shapes=(), name=None, grid_spec=None):
    params = pltpu.CompilerParams(dimension_semantics=("arbitrary",) * len(grid), vmem_limit_bytes=VMEM_LIMIT)
    if grid_spec is not None:
        return pl.pallas_call(kernel, grid_spec=grid_spec, out_shape=out_shape, compiler_params=params, name=name)
    return pl.pallas_call(kernel, grid=grid, in_specs=in_specs, out_specs=out_specs, out_shape=out_shape,
                          scratch_shapes=scratch_shapes, compiler_params=params, name=name)


def _embed_kernel(xp_ref, xs_ref, g_ref, h_ref, n_ref, *, n_prompt_blocks):
    x = jnp.where(pl.program_id(0) < n_prompt_blocks, xp_ref[...], xs_ref[...])
    h_ref[...] = x
    n_ref[...] = _rms(x, g_ref[...]).astype(BF)


def _embed(x_prompt, x_sample, g):
    mp, d = x_prompt.shape
    rb = x_sample.shape[0]
    assert mp % rb == 0 and rb % 16 == 0
    npb = mp // rb
    row = pl.BlockSpec((rb, d), lambda i: (i, 0))
    return _pcall(
        functools.partial(_embed_kernel, n_prompt_blocks=npb), grid=(npb + 1,),
        in_specs=[pl.BlockSpec((rb, d), lambda i: (jnp.minimum(i, npb - 1), 0)),
                  pl.BlockSpec((rb, d), lambda i: (0, 0)), pl.BlockSpec((1, d), lambda i: (0, 0))],
        out_specs=[row, row],
        out_shape=[jax.ShapeDtypeStruct((mp + rb, d), F32), jax.ShapeDtypeStruct((mp + rb, d), BF)],
        name="embed_norm")(x_prompt, x_sample, g.reshape(1, d))


def _post_kernel(f_ref, h_ref, gp_ref, gn_ref, ho_ref, no_ref, *, scale):
    h = h_ref[...] + scale * _rms(f_ref[...], gp_ref[...])
    ho_ref[...] = h
    no_ref[...] = _rms(h, gn_ref[...]).astype(BF)


def _post(f, h, g_post, g_next, scale):
    m, d = h.shape
    rb = _tile(m, 208, 16)
    row = pl.BlockSpec((rb, d), lambda i: (i, 0))
    vec = pl.BlockSpec((1, d), lambda i: (0, 0))
    return _pcall(
        functools.partial(_post_kernel, scale=scale), grid=(m // rb,),
        in_specs=[row, row, vec, vec], out_specs=[row, row],
        out_shape=[jax.ShapeDtypeStruct((m, d), F32), jax.ShapeDtypeStruct((m, d), BF)],
        name="post_norm")(f, h, g_post.reshape(1, d), g_next.reshape(1, d))


def _post_split_kernel(f_ref, h_ref, gp_ref, hp_ref, hs_ref, *, scale, n_prompt_blocks):
    h = h_ref[...] + scale * _rms(f_ref[...], gp_ref[...])
    i = pl.program_id(0)

    @pl.when(i < n_prompt_blocks)
    def _():
        hp_ref[...] = h

    @pl.when(i >= n_prompt_blocks)
    def _():
        hs_ref[...] = h


def _post_split(f, h, g_post, scale, mp):
    m, d = h.shape
    rb = m - mp
    assert mp % rb == 0 and rb % 16 == 0
    npb = mp // rb
    row = pl.BlockSpec((rb, d), lambda i: (i, 0))
    vec = pl.BlockSpec((1, d), lambda i: (0, 0))
    return _pcall(
        functools.partial(_post_split_kernel, scale=scale, n_prompt_blocks=npb), grid=(npb + 1,),
        in_specs=[row, row, vec],
        out_specs=[pl.BlockSpec((rb, d), lambda i: (jnp.minimum(i, npb - 1), 0)),
                   pl.BlockSpec((rb, d), lambda i: (0, 0))],
        out_shape=[jax.ShapeDtypeStruct((mp, d), F32), jax.ShapeDtypeStruct((rb, d), F32)],
        name="post_norm_out")(f, h, g_post.reshape(1, d))


def _mm_kernel(x_ref, w_ref, o_ref):
    o_ref[...] = _dot(x_ref[...], w_ref[...].astype(BF)).astype(o_ref.dtype)


def _x_tile(m, k):
    tm = _tile(m, X_TILE_BYTES // (2 * k), 16)
    return tm, pl.BlockSpec((tm, k), lambda i, j: (i, 0), pipeline_mode=pl.Buffered(1))


def _mm(x, w, layer, out_dtype, name):
    m, k = x.shape
    n = w.shape[-1]
    tm, x_spec = _x_tile(m, k)
    tn = _tile(n, 512, 2 * LANES) if n % (2 * LANES) == 0 else _tile(n, 512, LANES)
    if k > 8192:
        tn = _tile(n, 256, LANES)
    w_spec = pl.BlockSpec((None, k, tn), lambda i, j: (layer, 0, j))
    return _pcall(
        _mm_kernel, grid=(m // tm, n // tn), in_specs=[x_spec, w_spec],
        out_specs=pl.BlockSpec((tm, tn), lambda i, j: (i, j)),
        out_shape=jax.ShapeDtypeStruct((m, n), out_dtype), name=name)(x, w)


def _mm_nt_kernel(x_ref, w_ref, o_ref):
    o_ref[...] = _dot_nt(x_ref[...], w_ref[...].astype(BF)).astype(o_ref.dtype)


W_IN_BZ = 4096
W_IN_BR = W_IN_BZ + B_GATE_RANK
W_IN_KR = W_IN_BR + 3 * BRANCH_WIDTH + D_Q_RANK + D_KV_RANK
W_IN_GATE = W_IN_KR + D_ROPE


def _in_proj(x, w_t, layer, d):
    m, k = x.shape
    tm, x_spec = _x_tile(m, k)
    tn = 512
    run0 = W_IN_BZ // tn
    run1 = run0 + (W_IN_KR - W_IN_BR) // tn
    n_out = W_IN_BZ + (W_IN_KR - W_IN_BR) + N_BRANCH * d
    assert W_IN_BZ % tn == 0 and (W_IN_KR - W_IN_BR) % tn == 0 and (N_BRANCH * d) % tn == 0

    sub = 8
    assert W_IN_BR % sub == 0 and W_IN_GATE % sub == 0

    def w_row(j):
        return sub * jnp.where(j < run0, j * (tn // sub),
                               jnp.where(j < run1, W_IN_BR // sub + (j - run0) * (tn // sub),
                                         W_IN_GATE // sub + (j - run1) * (tn // sub)))

    def nt_kernel(x_ref, w_ref, o_ref):
        _mm_nt_kernel(x_ref, w_ref.at[0], o_ref)

    return _pcall(
        nt_kernel, grid=(m // tm, n_out // tn),
        in_specs=[x_spec,
                  pl.BlockSpec((pl.Element(1), pl.Element(tn), pl.Element(k)),
                               lambda i, j: (layer, w_row(j), 0))],
        out_specs=pl.BlockSpec((tm, tn), lambda i, j: (i, j)),
        out_shape=jax.ShapeDtypeStruct((m, n_out), F32), name="in_proj")(x, w_t)


def _in_small_kernel(x_ref, kr_ref, khi_ref, klo_ref, bz_ref, o_ref, w_scr):
    half = D_ROPE // 2

    @pl.when(pl.program_id(0) == 0)
    def _():
        w_scr[...] = jnp.zeros_like(w_scr)
        w_scr[S_KR:S_KR + D_ROPE, :] = kr_ref[0].astype(BF)
        w_scr[S_KRS:S_KRS + half, :] = khi_ref[0].astype(BF)
        w_scr[S_KRS + half:S_KRS + D_ROPE, :] = klo_ref[0].astype(BF)
        w_scr[S_BZ:S_BZ + B_GATE_RANK, :] = bz_ref[0].astype(BF)

    o_ref[...] = _dot_nt(x_ref[...], w_scr[...])


def _in_proj_small(x, w_t, layer):
    m, k = x.shape
    half = D_ROPE // 2
    tm = _tile(m, 832, 16)
    rows = lambda r0, nr: pl.BlockSpec((pl.Element(1), pl.Element(nr), pl.Element(k)), lambda i: (layer, r0, 0))
    return _pcall(
        _in_small_kernel, grid=(m // tm,),
        in_specs=[pl.BlockSpec((tm, k), lambda i: (i, 0)), rows(W_IN_KR, D_ROPE), rows(W_IN_KR + half, half),
                  rows(W_IN_KR, half), rows(W_IN_BZ, B_GATE_RANK)],
        out_specs=pl.BlockSpec((tm, S_WIDTH), lambda i: (i, 0)),
        out_shape=jax.ShapeDtypeStruct((m, S_WIDTH), F32),
        scratch_shapes=[pltpu.VMEM((S_WIDTH, k), BF)],
        name="in_proj_small")(x, w_t, w_t, w_t, w_t)


def _mm_gu_kernel(x_ref, wg_ref, wu_ref, o_ref):
    x = x_ref[...]
    a = _dot(x, wg_ref[...].astype(BF))
    b = _dot(x, wu_ref[...].astype(BF))
    o_ref[...] = (_silu(a) * b).astype(BF)


def _mm_gu(x, w_gu, layer):
    m, k = x.shape
    f = w_gu.shape[-1] // 2
    tm, x_spec = _x_tile(m, k)
    tn = _tile(f, 256, LANES)
    nj = f // tn
    return _pcall(
        _mm_gu_kernel, grid=(m // tm, nj),
        in_specs=[x_spec,
                  pl.BlockSpec((None, k, tn), lambda i, j: (layer, 0, j)),
                  pl.BlockSpec((None, k, tn), lambda i, j: (layer, 0, j + nj))],
        out_specs=pl.BlockSpec((tm, tn), lambda i, j: (i, j)),
        out_shape=jax.ShapeDtypeStruct((m, f), BF), name="ffn_gate_up")(x, w_gu, w_gu)


def _merge_kernel(ya_ref, yb_ref, yc_ref, yd_ref, w_ref, g0_ref, g1_ref, g2_ref, g3_ref, o_ref):
    acc = None
    for b, (y_ref, g_ref) in enumerate(zip((ya_ref, yb_ref, yc_ref, yd_ref), (g0_ref, g1_ref, g2_ref, g3_ref))):
        t = _dot(y_ref[...], w_ref[b].astype(BF)) * (0.5 * jnp.tanh(0.5 * g_ref[...]) + 0.5)
        acc = t if acc is None else acc + t
    o_ref[...] = acc.astype(BF)


def _merge(ys, w_branch, layer, proj, d):
    m = proj.shape[0]
    tm = _tile(m, X_TILE_BYTES // (2 * N_BRANCH * BRANCH_WIDTH), 16)
    tn = _tile(d, 256, LANES)
    y_spec = pl.BlockSpec((tm, BRANCH_WIDTH), lambda i, j: (i, 0), pipeline_mode=pl.Buffered(1))
    gate_specs = [pl.BlockSpec((tm, tn), functools.partial(lambda i, j, b: (i, (C_GATE + b * d) // tn + j), b=b))
                  for b in range(N_BRANCH)]
    return _pcall(
        _merge_kernel, grid=(m // tm, d // tn),
        in_specs=[y_spec] * N_BRANCH
        + [pl.BlockSpec((None, N_BRANCH, BRANCH_WIDTH, tn), lambda i, j: (layer, 0, 0, j))] + gate_specs,
        out_specs=pl.BlockSpec((tm, tn), lambda i, j: (i, j)),
        out_shape=jax.ShapeDtypeStruct((m, d), BF), name="gated_merge")(*ys, w_branch, proj, proj, proj, proj)


def _a_kernel(p_ref, lg_ref, lb_ref, ws_ref, bs_ref, y_ref, v_ref):
    u = _gelu(p_ref[:, :BRANCH_WIDTH])
    v = _layer_norm(_gelu(p_ref[:, BRANCH_WIDTH:]), lg_ref[...], lb_ref[...])
    v_ref[...] = v
    vb = v.astype(BF)
    gw = BRANCH_WIDTH // A_GROUPS
    for g in range(A_GROUPS):
        cols = slice(g * gw, (g + 1) * gw)
        mixed = _dot(ws_ref[g], vb[:, cols]) + bs_ref[g]
        y_ref[:, cols] = (u[:, cols] * mixed).astype(BF)


def _mixer_a(proj, n_prompt_chunks, ln_g, ln_b, w_s, b_s):
    m = proj.shape[0]
    c = A_CHUNK
    causal = jnp.tril(jnp.ones((c, c), dtype=bool))
    ws = jnp.stack([jnp.where(causal, w_s, 0.0),
                    w_s[:, :1, :1] * jnp.eye(c, dtype=F32)[None]]).astype(BF)
    bs = jnp.stack([b_s, jnp.broadcast_to(b_s[:, :1], b_s.shape)])[..., None]
    kind = lambda i: i // n_prompt_chunks
    y, v = _pcall(
        _a_kernel, grid=(m // c,),
        in_specs=[pl.BlockSpec((c, 2 * BRANCH_WIDTH), lambda i: (i, C_AU // (2 * BRANCH_WIDTH))),
                  pl.BlockSpec((1, BRANCH_WIDTH), lambda i: (0, 0)),
                  pl.BlockSpec((1, BRANCH_WIDTH), lambda i: (0, 0)),
                  pl.BlockSpec((None, A_GROUPS, c, c), lambda i: (kind(i), 0, 0, 0)),
                  pl.BlockSpec((None, A_GROUPS, c, 1), lambda i: (kind(i), 0, 0, 0))],
        out_specs=[pl.BlockSpec((c, BRANCH_WIDTH), lambda i: (i, 0)),
                   pl.BlockSpec((c, BRANCH_WIDTH), lambda i: (0, 0))],
        out_shape=[jax.ShapeDtypeStruct((m, BRANCH_WIDTH), BF), jax.ShapeDtypeStruct((c, BRANCH_WIDTH), F32)],
        name="mixer_a")(proj, ln_g.reshape(1, -1), ln_b.reshape(1, -1), ws, bs)
    return y, v


def _gla_log_decay(z_ref, wa_ref, ba_ref, rows):
    z = z_ref[rows, :].astype(BF)
    return _log_sigmoid(_dot(z, wa_ref[...]) + ba_ref[...]) * (1.0 / B_GATE_TAU)


def _gla_out(o, g, r):
    return _silu(r) * _rms(o, g)


def _gla_p_kernel(q_ref, k_ref, v_ref, r_ref, z_ref, wa_ref, ba_ref, g_ref, y_ref, so_ref, s_scr, *, n_chunks):
    blk = pl.program_id(1)
    c = B_CHUNK

    @pl.when(blk == 0)
    def _():
        s_scr[...] = jnp.zeros_like(s_scr)

    ri = lax.broadcasted_iota(jnp.int32, (c, c), 0)
    ci = lax.broadcasted_iota(jnp.int32, (c, c), 1)
    causal = ri >= ci
    tri = jnp.where(causal, 1.0, 0.0).astype(BF)
    ones = jnp.ones((c, B_DV), BF)

    def chunk(ic, carry):
        rows = pl.ds(pl.multiple_of(ic * c, c), c)
        la_all = _gla_log_decay(z_ref, wa_ref, ba_ref, rows)
        for h in range(B_HEADS):
            ks = slice(h * B_DK, (h + 1) * B_DK)
            vs = slice(h * B_DV, (h + 1) * B_DV)
            la = la_all[:, ks]
            hi, mid, lo = _split3(la)
            cum = _dot(tri, hi) + _dot(tri, mid) + _dot(tri, lo)
            tot_col = _dot_tn(hi, ones) + _dot_tn(mid, ones) + _dot_tn(lo, ones)
            q = q_ref[rows, ks] * (B_DK ** -0.5)
            k = k_ref[rows, ks]
            v = v_ref[rows, vs].astype(BF)
            q_t = (q * jnp.exp(cum)).astype(BF)
            k_t = (k * jnp.exp(-cum)).astype(BF)
            scores = jnp.where(causal, _dot_nt(q_t, k_t), 0.0)
            s = s_scr[h]
            o = _dot(scores.astype(BF), v) + _dot(q_t, s.astype(BF))
            total = cum[c - 1:c, :]
            k_end = (k * jnp.exp(total - cum)).astype(BF)
            s_scr[h] = jnp.exp(tot_col) * s + _dot_tn(k_end, v)
            y_ref[rows, vs] = _gla_out(o, g_ref[:, vs], r_ref[rows, vs]).astype(BF)
        return carry

    lax.fori_loop(0, n_chunks, chunk, 0, unroll=4)
    so_ref[...] = s_scr[...]


def _gla_alpha(w_alpha, b_alpha):
    wa = jnp.zeros((S_WIDTH, B_HEADS * B_DK), F32).at[S_BZ:S_BZ + B_GATE_RANK].set(w_alpha).astype(BF)
    return wa, b_alpha.reshape(1, -1)


def _gla_prompt(proj, small, bsz, t, w_alpha, b_alpha, norm_g):
    tb = _tile(t, 512, B_CHUNK)
    nb = t // tb
    hk = B_HEADS * B_DK
    hv = B_HEADS * B_DV
    wa, ba = _gla_alpha(w_alpha, b_alpha)
    row = lambda b, i: b * nb + i
    y, s = _pcall(
        functools.partial(_gla_p_kernel, n_chunks=tb // B_CHUNK), grid=(bsz, nb),
        in_specs=[pl.BlockSpec((tb, hk), lambda b, i: (row(b, i), C_BQ // hk)),
                  pl.BlockSpec((tb, hk), lambda b, i: (row(b, i), C_BK // hk)),
                  pl.BlockSpec((tb, hv), lambda b, i: (row(b, i), C_BV // hv)),
                  pl.BlockSpec((tb, hv), lambda b, i: (row(b, i), C_BR // hv)),
                  pl.BlockSpec((tb, S_WIDTH), lambda b, i: (row(b, i), 0)),
                  pl.BlockSpec((S_WIDTH, hk), lambda b, i: (0, 0)),
                  pl.BlockSpec((1, hk), lambda b, i: (0, 0)),
                  pl.BlockSpec((1, hv), lambda b, i: (0, 0))],
        out_specs=[pl.BlockSpec((tb, hv), lambda b, i: (row(b, i), 0)),
                   pl.BlockSpec((None, B_HEADS, B_DK, B_DV), lambda b, i: (b, 0, 0, 0))],
        out_shape=[jax.ShapeDtypeStruct((bsz * t, hv), BF),
                   jax.ShapeDtypeStruct((bsz, B_HEADS, B_DK, B_DV), F32)],
        scratch_shapes=[pltpu.VMEM((B_HEADS, B_DK, B_DV), F32)],
        name="gla_prompt")(proj, proj, proj, proj, small, wa, ba, norm_g.reshape(1, -1))
    return y, s


def _gla_s_kernel(q_ref, k_ref, v_ref, r_ref, z_ref, wa_ref, ba_ref, g_ref, s_ref, y_ref, so_ref, *, bb):
    la_all = _gla_log_decay(z_ref, wa_ref, ba_ref, slice(None))
    ri = lax.broadcasted_iota(jnp.int32, (B_DK, B_DK), 0)
    ci = lax.broadcasted_iota(jnp.int32, (B_DK, B_DK), 1)
    eye = ri == ci
    ones = jnp.ones((B_DK, B_DV), BF)

    def diag(row):
        return jnp.where(eye, jnp.broadcast_to(row, (B_DK, B_DK)), 0.0)

    for b in range(bb):
        rs = slice(b, b + 1)
        for h in range(B_HEADS):
            ks = slice(h * B_DK, (h + 1) * B_DK)
            vs = slice(h * B_DV, (h + 1) * B_DV)
            la = la_all[rs, ks]
            k = k_ref[rs, ks]
            v = v_ref[rs, vs]
            q_t = q_ref[rs, ks] * (B_DK ** -0.5) * jnp.exp(la)
            k_t = k * jnp.exp(-la)
            score = jnp.sum(q_t * k_t, axis=-1, keepdims=True)
            s0 = s_ref[b, h]
            q8 = jnp.broadcast_to(q_t, (8, B_DK)).astype(BF)
            o = score * v + _dot(q8, s0.astype(BF))[0:1]
            hi, mid, lo = _split3(diag(la))
            la_col = _dot(hi, ones) + _dot(mid, ones) + _dot(lo, ones)
            kv = _dot(diag(k).astype(BF), jnp.broadcast_to(v, (B_DK, B_DV)).astype(BF))
            so_ref[b, h] = jnp.exp(la_col) * s0 + kv
            y_ref[rs, vs] = _gla_out(o, g_ref[:, vs], r_ref[rs, vs])


def _gla_sample(proj, small, row0, n, w_alpha, b_alpha, norm_g, state, layer):
    bb = 8
    hk = B_HEADS * B_DK
    hv = B_HEADS * B_DV
    wa, ba = _gla_alpha(w_alpha, b_alpha)
    r0 = row0 // bb
    y, s = _pcall(
        functools.partial(_gla_s_kernel, bb=bb), grid=(n // bb,),
        in_specs=[pl.BlockSpec((bb, hk), lambda i: (r0 + i, C_BQ // hk)),
                  pl.BlockSpec((bb, hk), lambda i: (r0 + i, C_BK // hk)),
                  pl.BlockSpec((bb, hv), lambda i: (r0 + i, C_BV // hv)),
                  pl.BlockSpec((bb, hv), lambda i: (r0 + i, C_BR // hv)),
                  pl.BlockSpec((bb, S_WIDTH), lambda i: (r0 + i, 0)),
                  pl.BlockSpec((S_WIDTH, hk), lambda i: (0, 0)),
                  pl.BlockSpec((1, hk), lambda i: (0, 0)),
                  pl.BlockSpec((1, hv), lambda i: (0, 0)),
                  pl.BlockSpec((None, bb, B_HEADS, B_DK, B_DV), lambda i: (layer, i, 0, 0, 0))],
        out_specs=[pl.BlockSpec((bb, hv), lambda i: (i, 0)),
                   pl.BlockSpec((bb, B_HEADS, B_DK, B_DV), lambda i: (i, 0, 0, 0))],
        out_shape=[jax.ShapeDtypeStruct((n, hv), F32),
                   jax.ShapeDtypeStruct((n, B_HEADS, B_DK, B_DV), F32)],
        name="gla_sample")(proj, proj, proj, proj, small, wa, ba, norm_g.reshape(1, -1), state)
    return y.astype(BF), s


def _conv_p_kernel(a_ref, g_ref, ah_ref, gh_ref, w_ref, cb_ref, lg_ref, lb_ref, y_ref, st_ref, buf, conv, *, tb):
    i = pl.program_id(1)
    glu = a_ref[...] * jax.nn.sigmoid(g_ref[...])
    halo = ah_ref[...] * jax.nn.sigmoid(gh_ref[...])
    buf[0:CONV_HALO, :] = jnp.where(i > 0, halo, 0.0)
    buf[CONV_HALO:, :] = glu
    off = CONV_HALO - (CONV_WIDTH - 1)
    for c in range(buf.shape[1] // LANES):
        cs = slice(c * LANES, (c + 1) * LANES)
        acc = jnp.broadcast_to(cb_ref[:, cs], (tb, LANES))
        for j in range(CONV_WIDTH):
            acc = acc + w_ref[j:j + 1, cs] * buf[off + j:off + j + tb, cs]
        conv[:, cs] = acc
    y_ref[...] = _silu(_layer_norm(conv[...], lg_ref[...], lb_ref[...])).astype(BF)
    st_ref[...] = glu[tb - CONV_HALO:, :]


def _conv_prompt(proj, bsz, t, conv_w, conv_b, ln_g, ln_b):
    w = BRANCH_WIDTH
    tb = _tile(t, 256, CONV_HALO)
    nb = t // tb
    per = tb // CONV_HALO
    row = lambda b, i: b * nb + i
    halo = lambda b, i: jnp.maximum((b * nb + i) * per - 1, 0)
    vec = pl.BlockSpec((1, w), lambda b, i: (0, 0))
    y, st = _pcall(
        functools.partial(_conv_p_kernel, tb=tb), grid=(bsz, nb),
        in_specs=[pl.BlockSpec((tb, w), lambda b, i: (row(b, i), C_CA // w)),
                  pl.BlockSpec((tb, w), lambda b, i: (row(b, i), C_CG // w)),
                  pl.BlockSpec((CONV_HALO, w), lambda b, i: (halo(b, i), C_CA // w)),
                  pl.BlockSpec((CONV_HALO, w), lambda b, i: (halo(b, i), C_CG // w)),
                  pl.BlockSpec((CONV_WIDTH, w), lambda b, i: (0, 0)), vec, vec, vec],
        out_specs=[pl.BlockSpec((tb, w), lambda b, i: (row(b, i), 0)),
                   pl.BlockSpec((None, CONV_HALO, w), lambda b, i: (b, 0, 0))],
        out_shape=[jax.ShapeDtypeStruct((bsz * t, w), BF), jax.ShapeDtypeStruct((bsz, CONV_HALO, w), F32)],
        scratch_shapes=[pltpu.VMEM((tb + CONV_HALO, w), F32), pltpu.VMEM((tb, w), F32)],
        name="conv_prompt")(proj, proj, proj, proj, conv_w, conv_b.reshape(1, w), ln_g.reshape(1, w),
                            ln_b.reshape(1, w))
    return y, st[:, CONV_HALO - (CONV_WIDTH - 1):, :]


def _conv_s_kernel(a_ref, g_ref, st_ref, w_ref, cb_ref, lg_ref, lb_ref, y_ref, so_ref):
    nbuf = CONV_WIDTH - 1
    glu = a_ref[...] * jax.nn.sigmoid(g_ref[...])
    acc = cb_ref[...] + w_ref[nbuf:nbuf + 1, :] * glu
    for j in range(nbuf):
        acc = acc + w_ref[j:j + 1, :] * st_ref[j]
    for j in range(nbuf - 1):
        so_ref[j] = st_ref[j + 1]
    so_ref[nbuf - 1] = glu
    y_ref[...] = _silu(_layer_norm(acc, lg_ref[...], lb_ref[...]))


def _conv_sample(proj, row0, n, state, layer, conv_w, conv_b, ln_g, ln_b):
    w = BRANCH_WIDTH
    nbuf = CONV_WIDTH - 1
    bb = _tile(n, 32, 8)
    r0 = row0 // bb
    st = jnp.transpose(state, (0, 2, 1, 3))
    vec = pl.BlockSpec((1, w), lambda i: (0, 0))
    y, so = _pcall(
        _conv_s_kernel, grid=(n // bb,),
        in_specs=[pl.BlockSpec((bb, w), lambda i: (r0 + i, C_CA // w)),
                  pl.BlockSpec((bb, w), lambda i: (r0 + i, C_CG // w)),
                  pl.BlockSpec((None, nbuf, bb, w), lambda i: (layer, 0, i, 0)),
                  pl.BlockSpec((CONV_WIDTH, w), lambda i: (0, 0)), vec, vec, vec],
        out_specs=[pl.BlockSpec((bb, w), lambda i: (i, 0)),
                   pl.BlockSpec((nbuf, bb, w), lambda i: (0, i, 0))],
        out_shape=[jax.ShapeDtypeStruct((n, w), F32), jax.ShapeDtypeStruct((nbuf, n, w), F32)],
        name="conv_sample")(proj, proj, st, conv_w, conv_b.reshape(1, w), ln_g.reshape(1, w), ln_b.reshape(1, w))
    return y.astype(BF), jnp.transpose(so, (1, 0, 2))


def _qkv_kernel(pq_ref, sm_ref, ct_ref, st_ref, gq_ref, gkv_ref, wn_ref, wr_ref, wrs_ref, wuk_ref,
                lat_ref, kro_ref, kvb_ref, krb_ref, ql_ref, qr_ref):
    cq = _rms(pq_ref[:, :D_Q_RANK], gq_ref[...]).astype(BF)
    ckv = _rms(pq_ref[:, D_Q_RANK:], gkv_ref[...])
    lat_ref[...] = ckv
    kvb_ref[...] = ckv.astype(BF)
    cos = ct_ref[...]
    sin = st_ref[...]
    s = sm_ref[:, :LANES]
    kr = s * cos + pltpu.roll(s, D_ROPE, 1) * sin
    kro_ref[...] = kr
    krb_ref[...] = kr.astype(BF)
    qn = _dot(cq, wn_ref[...])
    q1 = _dot(cq, wr_ref[...])
    q2 = _dot(cq, wrs_ref[...])
    for h in range(D_HEADS):
        hs = slice(h * LANES, (h + 1) * LANES)
        ql_ref[h] = _dot(qn[:, hs].astype(BF), wuk_ref[h]).astype(BF)
        qr_ref[h] = (q1[:, hs] * cos + q2[:, hs] * sin).astype(BF)


def _rope_tables(pos):
    half = D_ROPE // 2
    inv_freq = ROPE_BASE ** (-jnp.arange(half, dtype=F32) / half)
    ang = pos.astype(F32)[:, None] * inv_freq[None, :]
    cos = jnp.cos(ang)
    sin = jnp.sin(ang)
    zero = jnp.zeros((pos.shape[0], LANES - D_ROPE), F32)
    return (jnp.concatenate([cos, cos, zero], axis=1), jnp.concatenate([-sin, sin, zero], axis=1))


def _qkv_prep(proj, small, cos_t, sin_t, gq, gkv, w_uq, w_uk):
    m = proj.shape[0]
    rb = _tile(m, 416, 16)
    half = D_ROPE // 2
    wq = w_uq.reshape(D_Q_RANK, D_HEADS, D_NOPE + D_ROPE)
    wn = wq[:, :, :D_NOPE].reshape(D_Q_RANK, D_HEADS * D_NOPE).astype(BF)
    rope = wq[:, :, D_NOPE:]
    pad = jnp.zeros((D_Q_RANK, D_HEADS, LANES - D_ROPE), F32)
    wr = jnp.concatenate([rope, pad], axis=-1).reshape(D_Q_RANK, D_HEADS * LANES).astype(BF)
    wrs = jnp.concatenate([rope[..., half:], rope[..., :half], pad], axis=-1).reshape(D_Q_RANK, D_HEADS * LANES).astype(BF)
    wuk = jnp.transpose(w_uk, (1, 2, 0)).astype(BF)
    blk = D_Q_RANK + D_KV_RANK
    row = lambda w: pl.BlockSpec((rb, w), lambda i: (i, 0))
    full2 = lambda a: pl.BlockSpec(a.shape, lambda i: (0, 0))
    return _pcall(
        _qkv_kernel, grid=(m // rb,),
        in_specs=[pl.BlockSpec((rb, blk), lambda i: (i, C_DQ // blk)), row(S_WIDTH), row(LANES), row(LANES),
                  pl.BlockSpec((1, D_Q_RANK), lambda i: (0, 0)), pl.BlockSpec((1, D_KV_RANK), lambda i: (0, 0)),
                  full2(wn), full2(wr), full2(wrs), pl.BlockSpec(wuk.shape, lambda i: (0, 0, 0))],
        out_specs=[row(D_KV_RANK), row(LANES), row(D_KV_RANK), row(LANES),
                   pl.BlockSpec((D_HEADS, rb, D_KV_RANK), lambda i: (0, i, 0)),
                   pl.BlockSpec((D_HEADS, rb, LANES), lambda i: (0, i, 0))],
        out_shape=[jax.ShapeDtypeStruct((m, D_KV_RANK), F32), jax.ShapeDtypeStruct((m, LANES), F32),
                   jax.ShapeDtypeStruct((m, D_KV_RANK), BF), jax.ShapeDtypeStruct((m, LANES), BF),
                   jax.ShapeDtypeStruct((D_HEADS, m, D_KV_RANK), BF),
                   jax.ShapeDtypeStruct((D_HEADS, m, LANES), BF)],
        name="mla_qkv_prep")(proj, small, cos_t, sin_t, gq.reshape(1, -1), gkv.reshape(1, -1), wn, wr, wrs, wuk)


def _attn_p_kernel(ql_ref, qr_ref, kv_ref, kr_ref, o_ref, m_scr, l_scr, acc_scr, *, tq):
    i = pl.program_id(1)
    scale = (D_NOPE + D_ROPE) ** -0.5
    m_scr[...] = jnp.full_like(m_scr, NEG_INF)
    l_scr[...] = jnp.zeros_like(l_scr)
    acc_scr[...] = jnp.zeros_like(acc_scr)
    q_pos = i * tq + lax.broadcasted_iota(jnp.int32, (tq, tq), 0)
    k_off = lax.broadcasted_iota(jnp.int32, (tq, tq), 1)

    def key_block(j, carry):
        rows = pl.ds(pl.multiple_of(j * tq, tq), tq)
        kv = kv_ref[rows, :]
        kr = kr_ref[rows, :]
        visible = j * tq + k_off <= q_pos
        for h in range(D_HEADS):
            s = (_dot_nt(ql_ref[h], kv) + _dot_nt(qr_ref[h], kr)) * scale
            s = jnp.where(visible, s, NEG_INF)
            m_prev = m_scr[h]
            m_new = jnp.maximum(m_prev, jnp.max(s, axis=-1, keepdims=True))
            alpha = jnp.exp(m_prev - m_new)
            e = jnp.exp(s - m_new)
            l_scr[h] = alpha * l_scr[h] + jnp.sum(e, axis=-1, keepdims=True)
            acc_scr[h] = alpha * acc_scr[h] + _dot(e.astype(BF), kv)
            m_scr[h] = m_new
        return carry

    lax.fori_loop(0, i + 1, key_block, 0)
    for h in range(D_HEADS):
        o_ref[h] = (acc_scr[h] / l_scr[h]).astype(BF)


def _attn_prompt(ql, qr, kvb, krb, bsz, t):
    tq = _tile(t, 512, 16)
    nq = t // tq
    return _pcall(
        functools.partial(_attn_p_kernel, tq=tq), grid=(bsz, nq),
        in_specs=[pl.BlockSpec((D_HEADS, tq, D_KV_RANK), lambda b, i: (0, b * nq + i, 0)),
                  pl.BlockSpec((D_HEADS, tq, LANES), lambda b, i: (0, b * nq + i, 0)),
                  pl.BlockSpec((t, D_KV_RANK), lambda b, i: (b, 0)),
                  pl.BlockSpec((t, LANES), lambda b, i: (b, 0))],
        out_specs=pl.BlockSpec((D_HEADS, tq, D_KV_RANK), lambda b, i: (0, b * nq + i, 0)),
        out_shape=jax.ShapeDtypeStruct((D_HEADS, bsz * t, D_KV_RANK), BF),
        scratch_shapes=[pltpu.VMEM((D_HEADS, tq, 1), F32), pltpu.VMEM((D_HEADS, tq, 1), F32),
                        pltpu.VMEM((D_HEADS, tq, D_KV_RANK), F32)],
        name="mla_attn_prompt")(ql, qr, kvb, krb)


def _decode_kernel(pt_ref, q_ref, qr_ref, ckv_ref, krn_ref, lat_hbm, kr_hbm, o_ref,
                   lat_buf, kr_buf, kv_scr, kr_scr, sem, *, layer, pages, n_groups):
    b = pl.program_id(0)
    scale = (D_NOPE + D_ROPE) ** -0.5
    page = lat_buf.shape[2]

    def group_copies(seq, g, slot):
        copies = []
        for p in range(pages):
            idx = pt_ref[seq, g * pages + p]
            copies.append(pltpu.make_async_copy(lat_hbm.at[layer, idx], lat_buf.at[slot, p], sem.at[0, slot]))
            copies.append(pltpu.make_async_copy(kr_hbm.at[layer, idx], kr_buf.at[slot, p], sem.at[1, slot]))
        return copies

    @pl.when(b == 0)
    def _():
        for c in group_copies(0, 0, 0):
            c.start()

    q = q_ref[...]
    qr = qr_ref[:, :D_ROPE]
    m = jnp.full((D_HEADS, 1), NEG_INF, F32)
    l = jnp.zeros((D_HEADS, 1), F32)
    acc = jnp.zeros((D_HEADS, D_KV_RANK), F32)
    for g in range(n_groups):
        slot = g % 2
        if g + 1 < n_groups:
            for c in group_copies(b, g + 1, 1 - slot):
                c.start()
        else:
            @pl.when(b + 1 < pl.num_programs(0))
            def _():
                for c in group_copies(b + 1, 0, 1 - slot):
                    c.start()
        for c in group_copies(b, g, slot):
            c.wait()
        for p in range(pages):
            kv_scr[p * page:(p + 1) * page, :] = lat_buf[slot, p].astype(BF)
            kr_scr[:, p * page:(p + 1) * page] = kr_buf[slot, p].astype(BF)
        kv = kv_scr[...]
        s = (_dot_nt(q, kv) + _dot(qr, kr_scr[...])) * scale
        m_new = jnp.maximum(m, jnp.max(s, axis=-1, keepdims=True))
        alpha = jnp.exp(m - m_new)
        e = jnp.exp(s - m_new)
        l = alpha * l + jnp.sum(e, axis=-1, keepdims=True)
        acc = alpha * acc + _dot(e.astype(BF), kv)
        m = m_new

    ckv = ckv_ref[...]
    s_self = (jnp.sum(q.astype(F32) * ckv, axis=-1, keepdims=True)
              + jnp.sum(qr.astype(F32) * krn_ref[:, :D_ROPE], axis=-1, keepdims=True)) * scale
    m_new = jnp.maximum(m, s_self)
    alpha = jnp.exp(m - m_new)
    e_self = jnp.exp(s_self - m_new)
    o_ref[...] = (alpha * acc + e_self * ckv) / (alpha * l + e_self)


def _attn_decode(q8, qr8, cache_latent, cache_kr_t, layer, page_table, ckv_new, kr_new):
    n, n_tab = page_table.shape
    page = cache_latent.shape[2]
    pages = _tile(n_tab // 2, PAGES_PER_GROUP, 1)
    n_groups = n_tab // pages
    assert n_groups % 2 == 0
    seq = lambda last: pl.BlockSpec((None,) + last, lambda b, pt: (b, 0, 0))
    grid_spec = pltpu.PrefetchScalarGridSpec(
        num_scalar_prefetch=1, grid=(n,),
        in_specs=[seq((D_HEADS, D_KV_RANK)), seq((D_HEADS, LANES)), seq((1, D_KV_RANK)), seq((1, LANES)),
                  pl.BlockSpec(memory_space=pl.ANY), pl.BlockSpec(memory_space=pl.ANY)],
        out_specs=seq((D_HEADS, D_KV_RANK)),
        scratch_shapes=[pltpu.VMEM((2, pages, page, D_KV_RANK), F32), pltpu.VMEM((2, pages, D_ROPE, page), F32),
                        pltpu.VMEM((pages * page, D_KV_RANK), BF), pltpu.VMEM((D_ROPE, pages * page), BF),
                        pltpu.SemaphoreType.DMA((2, 2))])
    return _pcall(
        functools.partial(_decode_kernel, layer=layer, pages=pages, n_groups=n_groups), grid=(n,),
        in_specs=None, out_specs=None, out_shape=jax.ShapeDtypeStruct((n, D_HEADS, D_KV_RANK), F32),
        grid_spec=grid_spec, name="mla_decode")(
            page_table, q8, qr8, ckv_new.reshape(n, 1, D_KV_RANK), kr_new.reshape(n, 1, LANES),
            cache_latent, cache_kr_t)


def _uv_kernel(o_ref, w_ref, y_ref):
    for h in range(D_HEADS):
        y_ref[:, h * D_VDIM:(h + 1) * D_VDIM] = _dot(o_ref[h], w_ref[h]).astype(BF)


def _uv_proj(olat, w_uv):
    m = olat.shape[1]
    rb = _tile(m, 512, 16)
    wuv = jnp.transpose(w_uv, (1, 0, 2)).astype(BF)
    return _pcall(
        _uv_kernel, grid=(m // rb,),
        in_specs=[pl.BlockSpec((D_HEADS, rb, D_KV_RANK), lambda i: (0, i, 0)),
                  pl.BlockSpec(wuv.shape, lambda i: (0, 0, 0))],
        out_specs=pl.BlockSpec((rb, D_HEADS * D_VDIM), lambda i: (i, 0)),
        out_shape=jax.ShapeDtypeStruct((m, D_HEADS * D_VDIM), BF), name="mla_uv")(olat, wuv)


def kernel(x_prompt, x_sample, cache_latent, cache_k_rope, page_table, state_gla, state_conv, ffn1_norm_pre, ffn1_norm_post, ffn1_w_gu, ffn1_w_down, mix_norm_pre, mix_norm_post, w_in, a_ln_g, a_ln_b, a_w_s, a_b_s, b_w_alpha, b_b_alpha, b_norm_g, c_conv_w, c_conv_b, c_ln_g, c_ln_b, d_q_norm_g, d_kv_norm_g, d_w_uq, d_w_uk, d_w_uv, w_branch, w_out, ffn2_norm_pre, ffn2_norm_post, ffn2_w_gu, ffn2_w_down):
    bsz, t, d = x_prompt.shape
    n_s, t_s, _ = x_sample.shape
    depth = w_in.shape[0]
    assert t_s == 1 and n_s == A_CHUNK and t % A_CHUNK == 0
    mp = bsz * t
    past_len = page_table.shape[1] * cache_latent.shape[2]

    pos = jnp.concatenate([jnp.tile(jnp.arange(t, dtype=jnp.int32), bsz),
                           jnp.full((n_s,), past_len, jnp.int32)])
    cos_t, sin_t = _rope_tables(pos)
    w_in_t = jnp.swapaxes(w_in, 1, 2)
    cache_kr_t = jnp.swapaxes(cache_k_rope, 2, 3)

    h, n = _embed(x_prompt.reshape(mp, d), x_sample.reshape(n_s, d), ffn1_norm_pre[0])
    outs = [[] for _ in range(9)]
    for l in range(depth):
        hid = _mm_gu(n, ffn1_w_gu, l)
        f = _mm(hid, ffn1_w_down, l, F32, "ffn_down")
        h, n = _post(f, h, ffn1_norm_post[l], mix_norm_pre[l], 0.5)

        proj = _in_proj(n, w_in_t, l, d)
        small = _in_proj_small(n, w_in_t, l)

        y_a, v_s = _mixer_a(proj, mp // A_CHUNK, a_ln_g[l], a_ln_b[l], a_w_s[l], a_b_s[l])

        yb_p, gla_p = _gla_prompt(proj, small, bsz, t, b_w_alpha[l], b_b_alpha[l], b_norm_g[l])
        yb_s, gla_s = _gla_sample(proj, small, mp, n_s, b_w_alpha[l], b_b_alpha[l], b_norm_g[l], state_gla, l)
        y_b = jnp.concatenate([yb_p, yb_s], axis=0)

        yc_p, conv_p = _conv_prompt(proj, bsz, t, c_conv_w[l], c_conv_b[l], c_ln_g[l], c_ln_b[l])
        yc_s, conv_s = _conv_sample(proj, mp, n_s, state_conv, l, c_conv_w[l], c_conv_b[l], c_ln_g[l], c_ln_b[l])
        y_c = jnp.concatenate([yc_p, yc_s], axis=0)

        lat, kro, kvb, krb, ql, qr = _qkv_prep(proj, small, cos_t, sin_t, d_q_norm_g[l], d_kv_norm_g[l],
                                               d_w_uq[l], d_w_uk[l])
        o_p = _attn_prompt(ql, qr, kvb, krb, bsz, t)
        o_s = _attn_decode(jnp.transpose(ql[:, mp:], (1, 0, 2)), jnp.transpose(qr[:, mp:], (1, 0, 2)),
                           cache_latent, cache_kr_t, l, page_table, lat[mp:], kro[mp:])
        olat = jnp.concatenate([o_p, jnp.transpose(o_s, (1, 0, 2)).astype(BF)], axis=1)
        y_d = _uv_proj(olat, d_w_uv[l])

        merged = _merge((y_a, y_b, y_c, y_d), w_branch, l, proj, d)
        f = _mm(merged, w_out, l, F32, "out_proj")
        h, n = _post(f, h, mix_norm_post[l], ffn2_norm_pre[l], 1.0)

        hid = _mm_gu(n, ffn2_w_gu, l)
        f = _mm(hid, ffn2_w_down, l, F32, "ffn_down")
        if l + 1 < depth:
            h, n = _post(f, h, ffn2_norm_post[l], ffn1_norm_pre[l + 1], 0.5)
        else:
            y_p, y_s = _post_split(f, h, ffn2_norm_post[l], 0.5, mp)

        for acc, val in zip(outs, (lat[:mp].reshape(bsz, t, D_KV_RANK), lat[mp:].reshape(n_s, 1, D_KV_RANK),
                                   kro[:mp, :D_ROPE].reshape(bsz, t, D_ROPE), kro[mp:, :D_ROPE].reshape(n_s, 1, D_ROPE),
                                   gla_p, gla_s, conv_p, conv_s, v_s.reshape(n_s, 1, BRANCH_WIDTH))):
            acc.append(val)

    return (y_p.reshape(bsz, t, d), y_s.reshape(n_s, 1, d)) + tuple(jnp.stack(o) for o in outs)
```

```python
import functools

import jax
import jax.numpy as jnp
from jax import lax
from jax.experimental import pallas as pl
from jax.experimental.pallas import tpu as pltpu

F32 = jnp.float32
BF = jnp.bfloat16

EPS = 1e-6
N_BRANCH = 4
BRANCH_WIDTH = 1024
A_GROUPS = 4
A_CHUNK = 128
B_HEADS = 4
B_DK = 128
B_DV = 256
B_GATE_RANK = 16
B_GATE_TAU = 16.0
B_CHUNK = 64
CONV_WIDTH = 31
D_HEADS = 8
D_NOPE = 128
D_ROPE = 64
D_VDIM = 128
D_Q_RANK = 768
D_KV_RANK = 256
ROPE_BASE = 10000.0
NEG_INF = -1e30

LANES = 128
VMEM_LIMIT = 56 * 1024 * 1024
X_TILE_BYTES = 19 * 1024 * 1024
PAGES_PER_GROUP = 32
DECODE_SLOTS = 4
CONV_HALO = 32

C_AU, C_AV, C_BQ, C_BK, C_BV, C_BR, C_CA, C_CG, C_DQ, C_DKV, C_GATE = (
    0, 1024, 2048, 2560, 3072, 4096, 5120, 6144, 7168, 7936, 8192)
S_KR, S_KRS, S_BZ, S_WIDTH = 0, 64, 128, 256


def _tile(n, target, mult):
    best = None
    for t in range(mult, min(n, target) + 1, mult):
        if n % t == 0:
            best = t
    assert best is not None, (n, target, mult)
    return best


def _dot(a, b):
    return jnp.dot(a, b, preferred_element_type=F32)


def _dot_nt(a, b):
    return lax.dot_general(a, b, (((1,), (1,)), ((), ())), preferred_element_type=F32)


def _dot_tn(a, b):
    return lax.dot_general(a, b, (((0,), (0,)), ((), ())), preferred_element_type=F32)


def _split3(x):
    hi = x.astype(BF)
    r1 = x - hi.astype(F32)
    mid = r1.astype(BF)
    lo = (r1 - mid.astype(F32)).astype(BF)
    return hi, mid, lo


def _rms(x, g):
    return x * lax.rsqrt(jnp.mean(x * x, axis=-1, keepdims=True) + EPS) * g


def _layer_norm(x, g, b):
    mu = jnp.mean(x, axis=-1, keepdims=True)
    xc = x - mu
    var = jnp.mean(xc * xc, axis=-1, keepdims=True)
    return xc * lax.rsqrt(var + EPS) * g + b


def _silu(x):
    return x * jax.nn.sigmoid(x)


def _gelu(x):
    return jax.nn.gelu(x, approximate=True)


def _log_sigmoid(x):
    return jnp.minimum(x, 0.0) - jnp.log1p(jnp.exp(-jnp.abs(x)))


def _pcall(kernel, *, grid, in_specs, out_specs, out_shape, scratch_shapes=(), name=None, grid_spec=None):
    params = pltpu.CompilerParams(dimension_semantics=("arbitrary",) * len(grid), vmem_limit_bytes=VMEM_LIMIT)
    if grid_spec is not None:
        return pl.pallas_call(kernel, grid_spec=grid_spec, out_shape=out_shape, compiler_params=params, name=name)
    return pl.pallas_call(kernel, grid=grid, in_specs=in_specs, out_specs=out_specs, out_shape=out_shape,
                          scratch_shapes=scratch_shapes, compiler_params=params, name=name)


def _embed_kernel(xp_ref, xs_ref, g_ref, h_ref, n_ref, *, n_prompt_blocks):
    x = jnp.where(pl.program_id(0) < n_prompt_blocks, xp_ref[...], xs_ref[...])
    h_ref[...] = x
    n_ref[...] = _rms(x, g_ref[...]).astype(BF)


def _embed(x_prompt, x_sample, g):
    mp, d = x_prompt.shape
    rb = x_sample.shape[0]
    assert mp % rb == 0 and rb % 16 == 0
    npb = mp // rb
    row = pl.BlockSpec((rb, d), lambda i: (i, 0))
    return _pcall(
        functools.partial(_embed_kernel, n_prompt_blocks=npb), grid=(npb + 1,),
        in_specs=[pl.BlockSpec((rb, d), lambda i: (jnp.minimum(i, npb - 1), 0)),
                  pl.BlockSpec((rb, d), lambda i: (0, 0)), pl.BlockSpec((1, d), lambda i: (0, 0))],
        out_specs=[row, row],
        out_shape=[jax.ShapeDtypeStruct((mp + rb, d), F32), jax.ShapeDtypeStruct((mp + rb, d), BF)],
        name="embed_norm")(x_prompt, x_sample, g.reshape(1, d))


def _post_kernel(f_ref, h_ref, gp_ref, gn_ref, ho_ref, no_ref, *, scale):
    h = h_ref[...] + scale * _rms(f_ref[...], gp_ref[...])
    ho_ref[...] = h
    no_ref[...] = _rms(h, gn_ref[...]).astype(BF)


def _post(f, h, g_post, g_next, scale):
    m, d = h.shape
    rb = _tile(m, 208, 16)
    row = pl.BlockSpec((rb, d), lambda i: (i, 0))
    vec = pl.BlockSpec((1, d), lambda i: (0, 0))
    return _pcall(
        functools.partial(_post_kernel, scale=scale), grid=(m // rb,),
        in_specs=[row, row, vec, vec], out_specs=[row, row],
        out_shape=[jax.ShapeDtypeStruct((m, d), F32), jax.ShapeDtypeStruct((m, d), BF)],
        name="post_norm")(f, h, g_post.reshape(1, d), g_next.reshape(1, d))


def _post_split_kernel(f_ref, h_ref, gp_ref, hp_ref, hs_ref, *, scale, n_prompt_blocks):
    h = h_ref[...] + scale * _rms(f_ref[...], gp_ref[...])
    i = pl.program_id(0)

    @pl.when(i < n_prompt_blocks)
    def _():
        hp_ref[...] = h

    @pl.when(i >= n_prompt_blocks)
    def _():
        hs_ref[...] = h


def _post_split(f, h, g_post, scale, mp):
    m, d = h.shape
    rb = m - mp
    assert mp % rb == 0 and rb % 16 == 0
    npb = mp // rb
    row = pl.BlockSpec((rb, d), lambda i: (i, 0))
    vec = pl.BlockSpec((1, d), lambda i: (0, 0))
    return _pcall(
        functools.partial(_post_split_kernel, scale=scale, n_prompt_blocks=npb), grid=(npb + 1,),
        in_specs=[row, row, vec],
        out_specs=[pl.BlockSpec((rb, d), lambda i: (jnp.minimum(i, npb - 1), 0)),
                   pl.BlockSpec((rb, d), lambda i: (0, 0))],
        out_shape=[jax.ShapeDtypeStruct((mp, d), F32), jax.ShapeDtypeStruct((rb, d), F32)],
        name="post_norm_out")(f, h, g_post.reshape(1, d))


def _mm_kernel(x_ref, w_ref, o_ref):
    o_ref[...] = _dot(x_ref[...], w_ref[...].astype(BF)).astype(o_ref.dtype)


def _x_tile(m, k):
    tm = _tile(m, X_TILE_BYTES // (2 * k), 16)
    return tm, pl.BlockSpec((tm, k), lambda i, j: (i, 0), pipeline_mode=pl.Buffered(1))


def _mm(x, w, layer, out_dtype, name):
    m, k = x.shape
    n = w.shape[-1]
    tm, x_spec = _x_tile(m, k)
    tn = _tile(n, 512, 2 * LANES) if n % (2 * LANES) == 0 else _tile(n, 512, LANES)
    if k > 8192:
        tn = _tile(n, 256, LANES)
    w_spec = pl.BlockSpec((None, k, tn), lambda i, j: (layer, 0, j))
    return _pcall(
        _mm_kernel, grid=(m // tm, n // tn), in_specs=[x_spec, w_spec],
        out_specs=pl.BlockSpec((tm, tn), lambda i, j: (i, j)),
        out_shape=jax.ShapeDtypeStruct((m, n), out_dtype), name=name)(x, w)


def _mm_nt_kernel(x_ref, w_ref, o_ref):
    o_ref[...] = _dot_nt(x_ref[...], w_ref[...].astype(BF)).astype(o_ref.dtype)


W_IN_BZ = 4096
W_IN_BR = W_IN_BZ + B_GATE_RANK
W_IN_KR = W_IN_BR + 3 * BRANCH_WIDTH + D_Q_RANK + D_KV_RANK
W_IN_GATE = W_IN_KR + D_ROPE


def _in_proj(x, w_t, layer, d):
    m, k = x.shape
    tm, x_spec = _x_tile(m, k)
    tn = 512
    run0 = W_IN_BZ // tn
    run1 = run0 + (W_IN_KR - W_IN_BR) // tn
    n_out = W_IN_BZ + (W_IN_KR - W_IN_BR) + N_BRANCH * d
    assert W_IN_BZ % tn == 0 and (W_IN_KR - W_IN_BR) % tn == 0 and (N_BRANCH * d) % tn == 0

    sub = 8
    assert W_IN_BR % sub == 0 and W_IN_GATE % sub == 0

    def w_row(j):
        return sub * jnp.where(j < run0, j * (tn // sub),
                               jnp.where(j < run1, W_IN_BR // sub + (j - run0) * (tn // sub),
                                         W_IN_GATE // sub + (j - run1) * (tn // sub)))

    def nt_kernel(x_ref, w_ref, o_ref):
        _mm_nt_kernel(x_ref, w_ref.at[0], o_ref)

    return _pcall(
        nt_kernel, grid=(m // tm, n_out // tn),
        in_specs=[x_spec,
                  pl.BlockSpec((pl.Element(1), pl.Element(tn), pl.Element(k)),
                               lambda i, j: (layer, w_row(j), 0))],
        out_specs=pl.BlockSpec((tm, tn), lambda i, j: (i, j)),
        out_shape=jax.ShapeDtypeStruct((m, n_out), F32), name="in_proj")(x, w_t)


def _in_small_kernel(x_ref, kr_ref, khi_ref, klo_ref, bz_ref, o_ref, w_scr):
    half = D_ROPE // 2

    @pl.when(pl.program_id(0) == 0)
    def _():
        w_scr[...] = jnp.zeros_like(w_scr)
        w_scr[S_KR:S_KR + D_ROPE, :] = kr_ref[0].astype(BF)
        w_scr[S_KRS:S_KRS + half, :] = khi_ref[0].astype(BF)
        w_scr[S_KRS + half:S_KRS + D_ROPE, :] = klo_ref[0].astype(BF)
        w_scr[S_BZ:S_BZ + B_GATE_RANK, :] = bz_ref[0].astype(BF)

    o_ref[...] = _dot_nt(x_ref[...], w_scr[...])


def _in_proj_small(x, w_t, layer):
    m, k = x.shape
    half = D_ROPE // 2
    tm = _tile(m, 832, 16)
    rows = lambda r0, nr: pl.BlockSpec((pl.Element(1), pl.Element(nr), pl.Element(k)), lambda i: (layer, r0, 0))
    return _pcall(
        _in_small_kernel, grid=(m // tm,),
        in_specs=[pl.BlockSpec((tm, k), lambda i: (i, 0)), rows(W_IN_KR, D_ROPE), rows(W_IN_KR + half, half),
                  rows(W_IN_KR, half), rows(W_IN_BZ, B_GATE_RANK)],
        out_specs=pl.BlockSpec((tm, S_WIDTH), lambda i: (i, 0)),
        out_shape=jax.ShapeDtypeStruct((m, S_WIDTH), F32),
        scratch_shapes=[pltpu.VMEM((S_WIDTH, k), BF)],
        name="in_proj_small")(x, w_t, w_t, w_t, w_t)


def _mm_gu_kernel(x_ref, wg_ref, wu_ref, o_ref):
    x = x_ref[...]
    a = _dot(x, wg_ref[...].astype(BF))
    b = _dot(x, wu_ref[...].astype(BF))
    o_ref[...] = (_silu(a) * b).astype(BF)


def _mm_gu(x, w_gu, layer):
    m, k = x.shape
    f = w_gu.shape[-1] // 2
    tm, x_spec = _x_tile(m, k)
    tn = _tile(f, 256, LANES)
    nj = f // tn
    return _pcall(
        _mm_gu_kernel, grid=(m // tm, nj),
        in_specs=[x_spec,
                  pl.BlockSpec((None, k, tn), lambda i, j: (layer, 0, j)),
                  pl.BlockSpec((None, k, tn), lambda i, j: (layer, 0, j + nj))],
        out_specs=pl.BlockSpec((tm, tn), lambda i, j: (i, j)),
        out_shape=jax.ShapeDtypeStruct((m, f), BF), name="ffn_gate_up")(x, w_gu, w_gu)


def _merge_kernel(ya_ref, yb_ref, yc_ref, yd_ref, w_ref, g0_ref, g1_ref, g2_ref, g3_ref, o_ref):
    acc = None
    for b, (y_ref, g_ref) in enumerate(zip((ya_ref, yb_ref, yc_ref, yd_ref), (g0_ref, g1_ref, g2_ref, g3_ref))):
        t = _dot(y_ref[...], w_ref[b].astype(BF)) * (0.5 * jnp.tanh(0.5 * g_ref[...]) + 0.5)
        acc = t if acc is None else acc + t
    o_ref[...] = acc.astype(BF)


def _merge(ys, w_branch, layer, proj, d):
    m = proj.shape[0]
    tm = _tile(m, X_TILE_BYTES // (2 * N_BRANCH * BRANCH_WIDTH), 16)
    tn = _tile(d, 256, LANES)
    y_spec = pl.BlockSpec((tm, BRANCH_WIDTH), lambda i, j: (i, 0), pipeline_mode=pl.Buffered(1))
    gate_specs = [pl.BlockSpec((tm, tn), functools.partial(lambda i, j, b: (i, (C_GATE + b * d) // tn + j), b=b))
                  for b in range(N_BRANCH)]
    return _pcall(
        _merge_kernel, grid=(m // tm, d // tn),
        in_specs=[y_spec] * N_BRANCH
        + [pl.BlockSpec((None, N_BRANCH, BRANCH_WIDTH, tn), lambda i, j: (layer, 0, 0, j))] + gate_specs,
        out_specs=pl.BlockSpec((tm, tn), lambda i, j: (i, j)),
        out_shape=jax.ShapeDtypeStruct((m, d), BF), name="gated_merge")(*ys, w_branch, proj, proj, proj, proj)


def _a_kernel(p_ref, lg_ref, lb_ref, ws_ref, bs_ref, y_ref, v_ref):
    u = _gelu(p_ref[:, :BRANCH_WIDTH])
    v = _layer_norm(_gelu(p_ref[:, BRANCH_WIDTH:]), lg_ref[...], lb_ref[...])
    v_ref[...] = v
    vb = v.astype(BF)
    gw = BRANCH_WIDTH // A_GROUPS
    for g in range(A_GROUPS):
        cols = slice(g * gw, (g + 1) * gw)
        mixed = _dot(ws_ref[g], vb[:, cols]) + bs_ref[g]
        y_ref[:, cols] = (u[:, cols] * mixed).astype(BF)


def _mixer_a(proj, n_prompt_chunks, ln_g, ln_b, w_s, b_s):
    m = proj.shape[0]
    c = A_CHUNK
    causal = jnp.tril(jnp.ones((c, c), dtype=bool))
    ws = jnp.stack([jnp.where(causal, w_s, 0.0),
                    w_s[:, :1, :1] * jnp.eye(c, dtype=F32)[None]]).astype(BF)
    bs = jnp.stack([b_s, jnp.broadcast_to(b_s[:, :1], b_s.shape)])[..., None]
    kind = lambda i: i // n_prompt_chunks
    y, v = _pcall(
        _a_kernel, grid=(m // c,),
        in_specs=[pl.BlockSpec((c, 2 * BRANCH_WIDTH), lambda i: (i, C_AU // (2 * BRANCH_WIDTH))),
                  pl.BlockSpec((1, BRANCH_WIDTH), lambda i: (0, 0)),
                  pl.BlockSpec((1, BRANCH_WIDTH), lambda i: (0, 0)),
                  pl.BlockSpec((None, A_GROUPS, c, c), lambda i: (kind(i), 0, 0, 0)),
                  pl.BlockSpec((None, A_GROUPS, c, 1), lambda i: (kind(i), 0, 0, 0))],
        out_specs=[pl.BlockSpec((c, BRANCH_WIDTH), lambda i: (i, 0)),
                   pl.BlockSpec((c, BRANCH_WIDTH), lambda i: (0, 0))],
        out_shape=[jax.ShapeDtypeStruct((m, BRANCH_WIDTH), BF), jax.ShapeDtypeStruct((c, BRANCH_WIDTH), F32)],
        name="mixer_a")(proj, ln_g.reshape(1, -1), ln_b.reshape(1, -1), ws, bs)
    return y, v


def _gla_log_decay(z_ref, wa_ref, ba_ref, rows):
    z = z_ref[rows, :].astype(BF)
    return _log_sigmoid(_dot(z, wa_ref[...]) + ba_ref[...]) * (1.0 / B_GATE_TAU)


def _gla_out(o, g, r):
    return _silu(r) * _rms(o, g)


def _gla_p_kernel(q_ref, k_ref, v_ref, r_ref, z_ref, wa_ref, ba_ref, g_ref, y_ref, so_ref, s_scr, *, n_chunks):
    blk = pl.program_id(1)
    c = B_CHUNK

    @pl.when(blk == 0)
    def _():
        s_scr[...] = jnp.zeros_like(s_scr)

    ri = lax.broadcasted_iota(jnp.int32, (c, c), 0)
    ci = lax.broadcasted_iota(jnp.int32, (c, c), 1)
    causal = ri >= ci
    tri = jnp.where(causal, 1.0, 0.0).astype(BF)
    ones = jnp.ones((c, B_DV), BF)

    def chunk(ic, carry):
        rows = pl.ds(pl.multiple_of(ic * c, c), c)
        la_all = _gla_log_decay(z_ref, wa_ref, ba_ref, rows)
        for h in range(B_HEADS):
            ks = slice(h * B_DK, (h + 1) * B_DK)
            vs = slice(h * B_DV, (h + 1) * B_DV)
            la = la_all[:, ks]
            hi, mid, lo = _split3(la)
            cum = _dot(tri, hi) + _dot(tri, mid) + _dot(tri, lo)
            tot_col = _dot_tn(hi, ones) + _dot_tn(mid, ones) + _dot_tn(lo, ones)
            q = q_ref[rows, ks] * (B_DK ** -0.5)
            k = k_ref[rows, ks]
            v = v_ref[rows, vs].astype(BF)
            q_t = (q * jnp.exp(cum)).astype(BF)
            k_t = (k * jnp.exp(-cum)).astype(BF)
            scores = jnp.where(causal, _dot_nt(q_t, k_t), 0.0)
            s = s_scr[h]
            o = _dot(scores.astype(BF), v) + _dot(q_t, s.astype(BF))
            total = cum[c - 1:c, :]
            k_end = (k * jnp.exp(total - cum)).astype(BF)
            s_scr[h] = jnp.exp(tot_col) * s + _dot_tn(k_end, v)
            y_ref[rows, vs] = _gla_out(o, g_ref[:, vs], r_ref[rows, vs]).astype(BF)
        return carry

    lax.fori_loop(0, n_chunks, chunk, 0, unroll=4)
    so_ref[...] = s_scr[...]


def _gla_alpha(w_alpha, b_alpha):
    wa = jnp.zeros((S_WIDTH, B_HEADS * B_DK), F32).at[S_BZ:S_BZ + B_GATE_RANK].set(w_alpha).astype(BF)
    return wa, b_alpha.reshape(1, -1)


def _gla_prompt(proj, small, bsz, t, w_alpha, b_alpha, norm_g):
    tb = _tile(t, 512, B_CHUNK)
    nb = t // tb
    hk = B_HEADS * B_DK
    hv = B_HEADS * B_DV
    wa, ba = _gla_alpha(w_alpha, b_alpha)
    row = lambda b, i: b * nb + i
    y, s = _pcall(
        functools.partial(_gla_p_kernel, n_chunks=tb // B_CHUNK), grid=(bsz, nb),
        in_specs=[pl.BlockSpec((tb, hk), lambda b, i: (row(b, i), C_BQ // hk)),
                  pl.BlockSpec((tb, hk), lambda b, i: (row(b, i), C_BK // hk)),
                  pl.BlockSpec((tb, hv), lambda b, i: (row(b, i), C_BV // hv)),
                  pl.BlockSpec((tb, hv), lambda b, i: (row(b, i), C_BR // hv)),
                  pl.BlockSpec((tb, S_WIDTH), lambda b, i: (row(b, i), 0)),
                  pl.BlockSpec((S_WIDTH, hk), lambda b, i: (0, 0)),
                  pl.BlockSpec((1, hk), lambda b, i: (0, 0)),
                  pl.BlockSpec((1, hv), lambda b, i: (0, 0))],
        out_specs=[pl.BlockSpec((tb, hv), lambda b, i: (row(b, i), 0)),
                   pl.BlockSpec((None, B_HEADS, B_DK, B_DV), lambda b, i: (b, 0, 0, 0))],
        out_shape=[jax.ShapeDtypeStruct((bsz * t, hv), BF),
                   jax.ShapeDtypeStruct((bsz, B_HEADS, B_DK, B_DV), F32)],
        scratch_shapes=[pltpu.VMEM((B_HEADS, B_DK, B_DV), F32)],
        name="gla_prompt")(proj, proj, proj, proj, small, wa, ba, norm_g.reshape(1, -1))
    return y, s


def _gla_s_kernel(q_ref, k_ref, v_ref, r_ref, z_ref, wa_ref, ba_ref, g_ref, s_ref, y_ref, so_ref, *, bb):
    la_all = _gla_log_decay(z_ref, wa_ref, ba_ref, slice(None))
    ri = lax.broadcasted_iota(jnp.int32, (B_DK, B_DK), 0)
    ci = lax.broadcasted_iota(jnp.int32, (B_DK, B_DK), 1)
    eye = ri == ci
    ones = jnp.ones((B_DK, B_DV), BF)

    def diag(row):
        return jnp.where(eye, jnp.broadcast_to(row, (B_DK, B_DK)), 0.0)

    for b in range(bb):
        rs = slice(b, b + 1)
        for h in range(B_HEADS):
            ks = slice(h * B_DK, (h + 1) * B_DK)
            vs = slice(h * B_DV, (h + 1) * B_DV)
            la = la_all[rs, ks]
            k = k_ref[rs, ks]
            v = v_ref[rs, vs]
            q_t = q_ref[rs, ks] * (B_DK ** -0.5) * jnp.exp(la)
            k_t = k * jnp.exp(-la)
            score = jnp.sum(q_t * k_t, axis=-1, keepdims=True)
            s0 = s_ref[b, h]
            q8 = jnp.broadcast_to(q_t, (8, B_DK)).astype(BF)
            o = score * v + _dot(q8, s0.astype(BF))[0:1]
            hi, mid, lo = _split3(diag(la))
            la_col = _dot(hi, ones) + _dot(mid, ones) + _dot(lo, ones)
            kv = _dot(diag(k).astype(BF), jnp.broadcast_to(v, (B_DK, B_DV)).astype(BF))
            so_ref[b, h] = jnp.exp(la_col) * s0 + kv
            y_ref[rs, vs] = _gla_out(o, g_ref[:, vs], r_ref[rs, vs])


def _gla_sample(proj, small, row0, n, w_alpha, b_alpha, norm_g, state, layer):
    bb = 8
    hk = B_HEADS * B_DK
    hv = B_HEADS * B_DV
    wa, ba = _gla_alpha(w_alpha, b_alpha)
    r0 = row0 // bb
    y, s = _pcall(
        functools.partial(_gla_s_kernel, bb=bb), grid=(n // bb,),
        in_specs=[pl.BlockSpec((bb, hk), lambda i: (r0 + i, C_BQ // hk)),
                  pl.BlockSpec((bb, hk), lambda i: (r0 + i, C_BK // hk)),
                  pl.BlockSpec((bb, hv), lambda i: (r0 + i, C_BV // hv)),
                  pl.BlockSpec((bb, hv), lambda i: (r0 + i, C_BR // hv)),
                  pl.BlockSpec((bb, S_WIDTH), lambda i: (r0 + i, 0)),
                  pl.BlockSpec((S_WIDTH, hk), lambda i: (0, 0)),
                  pl.BlockSpec((1, hk), lambda i: (0, 0)),
                  pl.BlockSpec((1, hv), lambda i: (0, 0)),
                  pl.BlockSpec((None, bb, B_HEADS, B_DK, B_DV), lambda i: (layer, i, 0, 0, 0))],
        out_specs=[pl.BlockSpec((bb, hv), lambda i: (i, 0)),
                   pl.BlockSpec((bb, B_HEADS, B_DK, B_DV), lambda i: (i, 0, 0, 0))],
        out_shape=[jax.ShapeDtypeStruct((n, hv), F32),
                   jax.ShapeDtypeStruct((n, B_HEADS, B_DK, B_DV), F32)],
        name="gla_sample")(proj, proj, proj, proj, small, wa, ba, norm_g.reshape(1, -1), state)
    return y.astype(BF), s


def _conv_p_kernel(a_ref, g_ref, ah_ref, gh_ref, w_ref, cb_ref, lg_ref, lb_ref, y_ref, st_ref, buf, conv, *, tb):
    i = pl.program_id(1)
    glu = a_ref[...] * jax.nn.sigmoid(g_ref[...])
    halo = ah_ref[...] * jax.nn.sigmoid(gh_ref[...])
    buf[0:CONV_HALO, :] = jnp.where(i > 0, halo, 0.0)
    buf[CONV_HALO:, :] = glu
    off = CONV_HALO - (CONV_WIDTH - 1)
    for c in range(buf.shape[1] // LANES):
        cs = slice(c * LANES, (c + 1) * LANES)
        acc = jnp.broadcast_to(cb_ref[:, cs], (tb, LANES))
        for j in range(CONV_WIDTH):
            acc = acc + w_ref[j:j + 1, cs] * buf[off + j:off + j + tb, cs]
        conv[:, cs] = acc
    y_ref[...] = _silu(_layer_norm(conv[...], lg_ref[...], lb_ref[...])).astype(BF)
    st_ref[...] = glu[tb - CONV_HALO:, :]


def _conv_prompt(proj, bsz, t, conv_w, conv_b, ln_g, ln_b):
    w = BRANCH_WIDTH
    tb = _tile(t, 256, CONV_HALO)
    nb = t // tb
    per = tb // CONV_HALO
    row = lambda b, i: b * nb + i
    halo = lambda b, i: jnp.maximum((b * nb + i) * per - 1, 0)
    vec = pl.BlockSpec((1, w), lambda b, i: (0, 0))
    y, st = _pcall(
        functools.partial(_conv_p_kernel, tb=tb), grid=(bsz, nb),
        in_specs=[pl.BlockSpec((tb, w), lambda b, i: (row(b, i), C_CA // w)),
                  pl.BlockSpec((tb, w), lambda b, i: (row(b, i), C_CG // w)),
                  pl.BlockSpec((CONV_HALO, w), lambda b, i: (halo(b, i), C_CA // w)),
                  pl.BlockSpec((CONV_HALO, w), lambda b, i: (halo(b, i), C_CG // w)),
                  pl.BlockSpec((CONV_WIDTH, w), lambda b, i: (0, 0)), vec, vec, vec],
        out_specs=[pl.BlockSpec((tb, w), lambda b, i: (row(b, i), 0)),
                   pl.BlockSpec((None, CONV_HALO, w), lambda b, i: (b, 0, 0))],
        out_shape=[jax.ShapeDtypeStruct((bsz * t, w), BF), jax.ShapeDtypeStruct((bsz, CONV_HALO, w), F32)],
        scratch_shapes=[pltpu.VMEM((tb + CONV_HALO, w), F32), pltpu.VMEM((tb, w), F32)],
        name="conv_prompt")(proj, proj, proj, proj, conv_w, conv_b.reshape(1, w), ln_g.reshape(1, w),
                            ln_b.reshape(1, w))
    return y, st[:, CONV_HALO - (CONV_WIDTH - 1):, :]


def _conv_s_kernel(a_ref, g_ref, st_ref, w_ref, cb_ref, lg_ref, lb_ref, y_ref, so_ref):
    nbuf = CONV_WIDTH - 1
    glu = a_ref[...] * jax.nn.sigmoid(g_ref[...])
    acc = cb_ref[...] + w_ref[nbuf:nbuf + 1, :] * glu
    for j in range(nbuf):
        acc = acc + w_ref[j:j + 1, :] * st_ref[j]
    for j in range(nbuf - 1):
        so_ref[j] = st_ref[j + 1]
    so_ref[nbuf - 1] = glu
    y_ref[...] = _silu(_layer_norm(acc, lg_ref[...], lb_ref[...]))


def _conv_sample(proj, row0, n, state, layer, conv_w, conv_b, ln_g, ln_b):
    w = BRANCH_WIDTH
    nbuf = CONV_WIDTH - 1
    bb = _tile(n, 32, 8)
    r0 = row0 // bb
    st = jnp.transpose(state, (0, 2, 1, 3))
    vec = pl.BlockSpec((1, w), lambda i: (0, 0))
    y, so = _pcall(
        _conv_s_kernel, grid=(n // bb,),
        in_specs=[pl.BlockSpec((bb, w), lambda i: (r0 + i, C_CA // w)),
                  pl.BlockSpec((bb, w), lambda i: (r0 + i, C_CG // w)),
                  pl.BlockSpec((None, nbuf, bb, w), lambda i: (layer, 0, i, 0)),
                  pl.BlockSpec((CONV_WIDTH, w), lambda i: (0, 0)), vec, vec, vec],
        out_specs=[pl.BlockSpec((bb, w), lambda i: (i, 0)),
                   pl.BlockSpec((nbuf, bb, w), lambda i: (0, i, 0))],
        out_shape=[jax.ShapeDtypeStruct((n, w), F32), jax.ShapeDtypeStruct((nbuf, n, w), F32)],
        name="conv_sample")(proj, proj, st, conv_w, conv_b.reshape(1, w), ln_g.reshape(1, w), ln_b.reshape(1, w))
    return y.astype(BF), jnp.transpose(so, (1, 0, 2))


def _qkv_kernel(pq_ref, sm_ref, ct_ref, st_ref, gq_ref, gkv_ref, wn_ref, wr_ref, wrs_ref, wuk_ref,
                lat_ref, kro_ref, kvb_ref, krb_ref, ql_ref, qr_ref):
    cq = _rms(pq_ref[:, :D_Q_RANK], gq_ref[...]).astype(BF)
    ckv = _rms(pq_ref[:, D_Q_RANK:], gkv_ref[...])
    lat_ref[...] = ckv
    kvb_ref[...] = ckv.astype(BF)
    cos = ct_ref[...]
    sin = st_ref[...]
    s = sm_ref[:, :LANES]
    kr = s * cos + pltpu.roll(s, D_ROPE, 1) * sin
    kro_ref[...] = kr
    krb_ref[...] = kr.astype(BF)
    qn = _dot(cq, wn_ref[...])
    q1 = _dot(cq, wr_ref[...])
    q2 = _dot(cq, wrs_ref[...])
    for h in range(D_HEADS):
        hs = slice(h * LANES, (h + 1) * LANES)
        ql_ref[h] = _dot(qn[:, hs].astype(BF), wuk_ref[h]).astype(BF)
        qr_ref[h] = (q1[:, hs] * cos + q2[:, hs] * sin).astype(BF)


def _rope_tables(pos):
    half = D_ROPE // 2
    inv_freq = ROPE_BASE ** (-jnp.arange(half, dtype=F32) / half)
    ang = pos.astype(F32)[:, None] * inv_freq[None, :]
    cos = jnp.cos(ang)
    sin = jnp.sin(ang)
    zero = jnp.zeros((pos.shape[0], LANES - D_ROPE), F32)
    return (jnp.concatenate([cos, cos, zero], axis=1), jnp.concatenate([-sin, sin, zero], axis=1))


def _qkv_prep(proj, small, cos_t, sin_t, gq, gkv, w_uq, w_uk):
    m = proj.shape[0]
    rb = _tile(m, 416, 16)
    half = D_ROPE // 2
    wq = w_uq.reshape(D_Q_RANK, D_HEADS, D_NOPE + D_ROPE)
    wn = wq[:, :, :D_NOPE].reshape(D_Q_RANK, D_HEADS * D_NOPE).astype(BF)
    rope = wq[:, :, D_NOPE:]
    pad = jnp.zeros((D_Q_RANK, D_HEADS, LANES - D_ROPE), F32)
    wr = jnp.concatenate([rope, pad], axis=-1).reshape(D_Q_RANK, D_HEADS * LANES).astype(BF)
    wrs = jnp.concatenate([rope[..., half:], rope[..., :half], pad], axis=-1).reshape(D_Q_RANK, D_HEADS * LANES).astype(BF)
    wuk = jnp.transpose(w_uk, (1, 2, 0)).astype(BF)
    blk = D_Q_RANK + D_KV_RANK
    row = lambda w: pl.BlockSpec((rb, w), lambda i: (i, 0))
    full2 = lambda a: pl.BlockSpec(a.shape, lambda i: (0, 0))
    return _pcall(
        _qkv_kernel, grid=(m // rb,),
        in_specs=[pl.BlockSpec((rb, blk), lambda i: (i, C_DQ // blk)), row(S_WIDTH), row(LANES), row(LANES),
                  pl.BlockSpec((1, D_Q_RANK), lambda i: (0, 0)), pl.BlockSpec((1, D_KV_RANK), lambda i: (0, 0)),
                  full2(wn), full2(wr), full2(wrs), pl.BlockSpec(wuk.shape, lambda i: (0, 0, 0))],
        out_specs=[row(D_KV_RANK), row(LANES), row(D_KV_RANK), row(LANES),
                   pl.BlockSpec((D_HEADS, rb, D_KV_RANK), lambda i: (0, i, 0)),
                   pl.BlockSpec((D_HEADS, rb, LANES), lambda i: (0, i, 0))],
        out_shape=[jax.ShapeDtypeStruct((m, D_KV_RANK), F32), jax.ShapeDtypeStruct((m, LANES), F32),
                   jax.ShapeDtypeStruct((m, D_KV_RANK), BF), jax.ShapeDtypeStruct((m, LANES), BF),
                   jax.ShapeDtypeStruct((D_HEADS, m, D_KV_RANK), BF),
                   jax.ShapeDtypeStruct((D_HEADS, m, LANES), BF)],
        name="mla_qkv_prep")(proj, small, cos_t, sin_t, gq.reshape(1, -1), gkv.reshape(1, -1), wn, wr, wrs, wuk)


def _attn_p_kernel(ql_ref, qr_ref, kv_ref, kr_ref, o_ref, m_scr, l_scr, acc_scr, *, tq):
    i = pl.program_id(1)
    scale = (D_NOPE + D_ROPE) ** -0.5
    m_scr[...] = jnp.full_like(m_scr, NEG_INF)
    l_scr[...] = jnp.zeros_like(l_scr)
    acc_scr[...] = jnp.zeros_like(acc_scr)
    q_pos = i * tq + lax.broadcasted_iota(jnp.int32, (tq, tq), 0)
    k_off = lax.broadcasted_iota(jnp.int32, (tq, tq), 1)

    def key_block(j, carry):
        rows = pl.ds(pl.multiple_of(j * tq, tq), tq)
        kv = kv_ref[rows, :]
        kr = kr_ref[rows, :]
        visible = j * tq + k_off <= q_pos
        for h in range(D_HEADS):
            s = (_dot_nt(ql_ref[h], kv) + _dot_nt(qr_ref[h], kr)) * scale
            s = jnp.where(visible, s, NEG_INF)
            m_prev = m_scr[h]
            m_new = jnp.maximum(m_prev, jnp.max(s, axis=-1, keepdims=True))
            alpha = jnp.exp(m_prev - m_new)
            e = jnp.exp(s - m_new)
            l_scr[h] = alpha * l_scr[h] + jnp.sum(e, axis=-1, keepdims=True)
            acc_scr[h] = alpha * acc_scr[h] + _dot(e.astype(BF), kv)
            m_scr[h] = m_new
        return carry

    lax.fori_loop(0, i + 1, key_block, 0)
    for h in range(D_HEADS):
        o_ref[h] = (acc_scr[h] / l_scr[h]).astype(BF)


def _attn_prompt(ql, qr, kvb, krb, bsz, t):
    tq = _tile(t, 512, 16)
    nq = t // tq
    return _pcall(
        functools.partial(_attn_p_kernel, tq=tq), grid=(bsz, nq),
        in_specs=[pl.BlockSpec((D_HEADS, tq, D_KV_RANK), lambda b, i: (0, b * nq + i, 0)),
                  pl.BlockSpec((D_HEADS, tq, LANES), lambda b, i: (0, b * nq + i, 0)),
                  pl.BlockSpec((t, D_KV_RANK), lambda b, i: (b, 0)),
                  pl.BlockSpec((t, LANES), lambda b, i: (b, 0))],
        out_specs=pl.BlockSpec((D_HEADS, tq, D_KV_RANK), lambda b, i: (0, b * nq + i, 0)),
        out_shape=jax.ShapeDtypeStruct((D_HEADS, bsz * t, D_KV_RANK), BF),
        scratch_shapes=[pltpu.VMEM((D_HEADS, tq, 1), F32), pltpu.VMEM((D_HEADS, tq, 1), F32),
                        pltpu.VMEM((D_HEADS, tq, D_KV_RANK), F32)],
        name="mla_attn_prompt")(ql, qr, kvb, krb)


def _decode_kernel(pt_ref, q_ref, qr_ref, ckv_ref, krn_ref, lat_hbm, kr_hbm, o_ref,
                   lat_buf, kr_buf, kv_scr, kr_scr, sem, *, layer, pages, n_groups):
    b = pl.program_id(0)
    scale = (D_NOPE + D_ROPE) ** -0.5
    page = lat_buf.shape[2]
    n_slots = lat_buf.shape[0]
    ahead = n_slots - 1

    def group_copies(seq, g, slot):
        copies = []
        for p in range(pages):
            idx = pt_ref[seq, g * pages + p]
            copies.append(pltpu.make_async_copy(lat_hbm.at[layer, idx], lat_buf.at[slot, p], sem.at[0, slot]))
            copies.append(pltpu.make_async_copy(kr_hbm.at[layer, idx], kr_buf.at[slot, p], sem.at[1, slot]))
        return copies

    def request(k):
        if k < n_groups:
            for c in group_copies(b, k, k % n_slots):
                c.start()
        else:
            @pl.when(b + 1 < pl.num_programs(0))
            def _():
                for c in group_copies(b + 1, k - n_groups, k % n_slots):
                    c.start()

    @pl.when(b == 0)
    def _():
        for k in range(ahead):
            for c in group_copies(0, k, k % n_slots):
                c.start()

    q = q_ref[...]
    qr = qr_ref[:, :D_ROPE]
    m = jnp.full((D_HEADS, 1), NEG_INF, F32)
    l = jnp.zeros((D_HEADS, 1), F32)
    acc = jnp.zeros((D_HEADS, D_KV_RANK), F32)
    for g in range(n_groups):
        slot = g % n_slots
        request(g + ahead)
        for c in group_copies(b, g, slot):
            c.wait()
        for p in range(pages):
            kv_scr[p * page:(p + 1) * page, :] = lat_buf[slot, p].astype(BF)
            kr_scr[:, p * page:(p + 1) * page] = kr_buf[slot, p].astype(BF)
        kv = kv_scr[...]
        s = (_dot_nt(q, kv) + _dot(qr, kr_scr[...])) * scale
        m_new = jnp.maximum(m, jnp.max(s, axis=-1, keepdims=True))
        alpha = jnp.exp(m - m_new)
        e = jnp.exp(s - m_new)
        l = alpha * l + jnp.sum(e, axis=-1, keepdims=True)
        acc = alpha * acc + _dot(e.astype(BF), kv)
        m = m_new

    ckv = ckv_ref[...]
    s_self = (jnp.sum(q.astype(F32) * ckv, axis=-1, keepdims=True)
              + jnp.sum(qr.astype(F32) * krn_ref[:, :D_ROPE], axis=-1, keepdims=True)) * scale
    m_new = jnp.maximum(m, s_self)
    alpha = jnp.exp(m - m_new)
    e_self = jnp.exp(s_self - m_new)
    o_ref[...] = (alpha * acc + e_self * ckv) / (alpha * l + e_self)


def _attn_decode(q8, qr8, cache_latent, cache_kr_t, layer, page_table, ckv_new, kr_new):
    n, n_tab = page_table.shape
    page = cache_latent.shape[2]
    pages = _tile(n_tab // DECODE_SLOTS, PAGES_PER_GROUP, 1)
    n_groups = n_tab // pages
    assert n_groups % DECODE_SLOTS == 0
    seq = lambda last: pl.BlockSpec((None,) + last, lambda b, pt: (b, 0, 0))
    grid_spec = pltpu.PrefetchScalarGridSpec(
        num_scalar_prefetch=1, grid=(n,),
        in_specs=[seq((D_HEADS, D_KV_RANK)), seq((D_HEADS, LANES)), seq((1, D_KV_RANK)), seq((1, LANES)),
                  pl.BlockSpec(memory_space=pl.ANY), pl.BlockSpec(memory_space=pl.ANY)],
        out_specs=seq((D_HEADS, D_KV_RANK)),
        scratch_shapes=[pltpu.VMEM((DECODE_SLOTS, pages, page, D_KV_RANK), F32),
                        pltpu.VMEM((DECODE_SLOTS, pages, D_ROPE, page), F32),
                        pltpu.VMEM((pages * page, D_KV_RANK), BF), pltpu.VMEM((D_ROPE, pages * page), BF),
                        pltpu.SemaphoreType.DMA((2, DECODE_SLOTS))])
    return _pcall(
        functools.partial(_decode_kernel, layer=layer, pages=pages, n_groups=n_groups), grid=(n,),
        in_specs=None, out_specs=None, out_shape=jax.ShapeDtypeStruct((n, D_HEADS, D_KV_RANK), F32),
        grid_spec=grid_spec, name="mla_decode")(
            page_table, q8, qr8, ckv_new.reshape(n, 1, D_KV_RANK), kr_new.reshape(n, 1, LANES),
            cache_latent, cache_kr_t)


def _uv_kernel(o_ref, w_ref, y_ref):
    for h in range(D_HEADS):
        y_ref[:, h * D_VDIM:(h + 1) * D_VDIM] = _dot(o_ref[h], w_ref[h]).astype(BF)


def _uv_proj(olat, w_uv):
    m = olat.shape[1]
    rb = _tile(m, 512, 16)
    wuv = jnp.transpose(w_uv, (1, 0, 2)).astype(BF)
    return _pcall(
        _uv_kernel, grid=(m // rb,),
        in_specs=[pl.BlockSpec((D_HEADS, rb, D_KV_RANK), lambda i: (0, i, 0)),
                  pl.BlockSpec(wuv.shape, lambda i: (0, 0, 0))],
        out_specs=pl.BlockSpec((rb, D_HEADS * D_VDIM), lambda i: (i, 0)),
        out_shape=jax.ShapeDtypeStruct((m, D_HEADS * D_VDIM), BF), name="mla_uv")(olat, wuv)


def kernel(x_prompt, x_sample, cache_latent, cache_k_rope, page_table, state_gla, state_conv, ffn1_norm_pre, ffn1_norm_post, ffn1_w_gu, ffn1_w_down, mix_norm_pre, mix_norm_post, w_in, a_ln_g, a_ln_b, a_w_s, a_b_s, b_w_alpha, b_b_alpha, b_norm_g, c_conv_w, c_conv_b, c_ln_g, c_ln_b, d_q_norm_g, d_kv_norm_g, d_w_uq, d_w_uk, d_w_uv, w_branch, w_out, ffn2_norm_pre, ffn2_norm_post, ffn2_w_gu, ffn2_w_down):
    bsz, t, d = x_prompt.shape
    n_s, t_s, _ = x_sample.shape
    depth = w_in.shape[0]
    assert t_s == 1 and n_s == A_CHUNK and t % A_CHUNK == 0
    mp = bsz * t
    past_len = page_table.shape[1] * cache_latent.shape[2]

    pos = jnp.concatenate([jnp.tile(jnp.arange(t, dtype=jnp.int32), bsz),
                           jnp.full((n_s,), past_len, jnp.int32)])
    cos_t, sin_t = _rope_tables(pos)
    w_in_t = jnp.swapaxes(w_in, 1, 2)
    cache_kr_t = jnp.swapaxes(cache_k_rope, 2, 3)

    h, n = _embed(x_prompt.reshape(mp, d), x_sample.reshape(n_s, d), ffn1_norm_pre[0])
    outs = [[] for _ in range(9)]
    for l in range(depth):
        hid = _mm_gu(n, ffn1_w_gu, l)
        f = _mm(hid, ffn1_w_down, l, F32, "ffn_down")
        h, n = _post(f, h, ffn1_norm_post[l], mix_norm_pre[l], 0.5)

        proj = _in_proj(n, w_in_t, l, d)
        small = _in_proj_small(n, w_in_t, l)

        y_a, v_s = _mixer_a(proj, mp // A_CHUNK, a_ln_g[l], a_ln_b[l], a_w_s[l], a_b_s[l])

        yb_p, gla_p = _gla_prompt(proj, small, bsz, t, b_w_alpha[l], b_b_alpha[l], b_norm_g[l])
        yb_s, gla_s = _gla_sample(proj, small, mp, n_s, b_w_alpha[l], b_b_alpha[l], b_norm_g[l], state_gla, l)
        y_b = jnp.concatenate([yb_p, yb_s], axis=0)

        yc_p, conv_p = _conv_prompt(proj, bsz, t, c_conv_w[l], c_conv_b[l], c_ln_g[l], c_ln_b[l])
        yc_s, conv_s = _conv_sample(proj, mp, n_s, state_conv, l, c_conv_w[l], c_conv_b[l], c_ln_g[l], c_ln_b[l])
        y_c = jnp.concatenate([yc_p, yc_s], axis=0)

        lat, kro, kvb, krb, ql, qr = _qkv_prep(proj, small, cos_t, sin_t, d_q_norm_g[l], d_kv_norm_g[l],
                                               d_w_uq[l], d_w_uk[l])
        o_p = _attn_prompt(ql, qr, kvb, krb, bsz, t)
        o_s = _attn_decode(jnp.transpose(ql[:, mp:], (1, 0, 2)), jnp.transpose(qr[:, mp:], (1, 0, 2)),
                           cache_latent, cache_kr_t, l, page_table, lat[mp:], kro[mp:])
        y_d = jnp.concatenate([_uv_proj(o_p, d_w_uv[l]),
                               _uv_proj(jnp.transpose(o_s, (1, 0, 2)).astype(BF), d_w_uv[l])], axis=0)

        merged = _merge((y_a, y_b, y_c, y_d), w_branch, l, proj, d)
        f = _mm(merged, w_out, l, F32, "out_proj")
        h, n = _post(f, h, mix_norm_post[l], ffn2_norm_pre[l], 1.0)

        hid = _mm_gu(n, ffn2_w_gu, l)
        f = _mm(hid, ffn2_w_down, l, F32, "ffn_down")
        if l + 1 < depth:
            h, n = _post(f, h, ffn2_norm_post[l], ffn1_norm_pre[l + 1], 0.5)
        else:
            y_p, y_s = _post_split(f, h, ffn2_norm_post[l], 0.5, mp)

        for acc, val in zip(outs, (lat[:mp].reshape(bsz, t, D_KV_RANK), lat[mp:].reshape(n_s, 1, D_KV_RANK),
                                   kro[:mp, :D_ROPE].reshape(bsz, t, D_ROPE), kro[mp:, :D_ROPE].reshape(n_s, 1, D_ROPE),
                                   gla_p, gla_s, conv_p, conv_s, v_s.reshape(n_s, 1, BRANCH_WIDTH))):
            acc.append(val)

    return (y_p.reshape(bsz, t, d), y_s.reshape(n_s, 1, d)) + tuple(jnp.stack(o) for o in outs)
```
